```python
import math
import jax, jax.numpy as jnp
from jax import lax
import numpy as np

D_MODEL = 1024
BATCH = 8
SEQ = 4096
DEPTH = 4

GRID_W = 64
CTX_LEN = 256
N_MIXERS = 3
D_FF = 4 * D_MODEL
NORM_EPS = 1e-6
N_MOD = 6
S5_GROUP = 16
S5_GROUPS = D_MODEL // S5_GROUP
S5_STATE = 64
S5_DT_MIN = 1e-3
S5_DT_MAX = 1e-1
DIFF_HEAD_DIM = 64
DIFF_HEADS = D_MODEL // (2 * DIFF_HEAD_DIM)
DIFF_V_DIM = 2 * DIFF_HEAD_DIM
ROPE_BASE = 10000.0
Q_BLOCK = 128
FOURIER_GROUPS = 4
FOURIER_CH = D_MODEL // FOURIER_GROUPS
N_S5_LAYERS = (DEPTH + 2) // 3
N_DIFF_LAYERS = (DEPTH + 1) // 3
N_FOURIER_LAYERS = DEPTH // 3

kernel_name = "hybrid_s5_diffattn_fourier_prefix_dit"


def _rmsnorm(x, g):
    x32 = x.astype(jnp.float32)
    y = x32 * lax.rsqrt(jnp.mean(x32 * x32, axis=-1, keepdims=True) + NORM_EPS)
    return (y * g.astype(jnp.float32)).astype(x.dtype)


def _modulate(h, shift, scale):
    return h * (1 + scale) + shift


def _sqrelu_mlp(h, w1, w2):
    a = jax.nn.relu(h @ w1)
    return (a * a) @ w2


def _s5_discretize(lam_re, lam_im, log_dt, b_re, b_im):
    f32 = jnp.float32
    lam_re = lam_re.astype(f32)
    lam_im = lam_im.astype(f32)
    dt = jnp.exp(log_dt.astype(f32))[:, None]
    mag = jnp.exp(lam_re * dt)
    a_re = mag * jnp.cos(lam_im * dt)
    a_im = mag * jnp.sin(lam_im * dt)
    n_re = a_re - 1.0
    n_im = a_im
    den = lam_re * lam_re + lam_im * lam_im
    k_re = (n_re * lam_re + n_im * lam_im) / den
    k_im = (n_im * lam_re - n_re * lam_im) / den
    b_re = b_re.astype(f32)
    b_im = b_im.astype(f32)
    bb_re = k_re[..., None] * b_re - k_im[..., None] * b_im
    bb_im = k_re[..., None] * b_im + k_im[..., None] * b_re
    return a_re, a_im, bb_re, bb_im


def _complex_combine(e1, e2):
    a1r, a1i, b1r, b1i = e1
    a2r, a2i, b2r, b2i = e2
    return (a2r * a1r - a2i * a1i,
            a2r * a1i + a2i * a1r,
            a2r * b1r - a2i * b1i + b2r,
            a2r * b1i + a2i * b1r + b2i)


def _s5_states(u, a_re, a_im, bb_re, bb_im, h0, reverse):
    bu_re = jnp.einsum('blgh,gph->blgp', u, bb_re)
    bu_im = jnp.einsum('blgh,gph->blgp', u, bb_im)
    if reverse:
        bu_re = jnp.flip(bu_re, 1)
        bu_im = jnp.flip(bu_im, 1)
    shape = (1, u.shape[1]) + a_re.shape
    ar = jnp.broadcast_to(a_re, shape)
    ai = jnp.broadcast_to(a_im, shape)
    acr, aci, hr, hi = lax.associative_scan(_complex_combine, (ar, ai, bu_re, bu_im), axis=1)
    if h0 is not None:
        h0r = h0[0][:, None]
        h0i = h0[1][:, None]
        hr = hr + acr * h0r - aci * h0i
        hi = hi + acr * h0i + aci * h0r
    final = (hr[:, -1], hi[:, -1])
    if reverse:
        hr = jnp.flip(hr, 1)
        hi = jnp.flip(hi, 1)
    return hr, hi, final


def _s5_readout(s_re, s_im, c_re, c_im):
    return jnp.einsum('blgp,ghp->blgh', s_re, c_re) - jnp.einsum('blgp,ghp->blgh', s_im, c_im)


def _s5_output(y, u, d, w_glu, dtype):
    bsz, n = y.shape[:2]
    z = jax.nn.gelu(y.reshape(bsz, n, D_MODEL) + d.astype(jnp.float32) * u.reshape(bsz, n, D_MODEL))
    g = z.astype(dtype) @ w_glu
    return g[..., :D_MODEL] * jax.nn.sigmoid(g[..., D_MODEL:])


def _s5_mixer(h_lat, h_ctx, lam_re, lam_im, log_dt, b_re, b_im, c_re, c_im, d, w_glu, ctx_out):
    f32 = jnp.float32
    bsz, n_lat, _ = h_lat.shape
    n_ctx = h_ctx.shape[1]
    u_lat = h_lat.astype(f32).reshape(bsz, n_lat, S5_GROUPS, S5_GROUP)
    u_ctx = h_ctx.astype(f32).reshape(bsz, n_ctx, S5_GROUPS, S5_GROUP)
    ys_lat = []
    ys_ctx = []
    for direction in range(2):
        reverse = direction == 1
        a_re, a_im, bb_re, bb_im = _s5_discretize(lam_re[direction], lam_im[direction], log_dt[direction], b_re[direction], b_im[direction])
        cr = c_re[direction].astype(f32)
        ci = c_im[direction].astype(f32)
        s_re, s_im, final = _s5_states(u_ctx, a_re, a_im, bb_re, bb_im, None, reverse)
        if ctx_out:
            ys_ctx.append(_s5_readout(s_re, s_im, cr, ci))
        s_re, s_im, _ = _s5_states(u_lat, a_re, a_im, bb_re, bb_im, final, reverse)
        ys_lat.append(_s5_readout(s_re, s_im, cr, ci))
    out_lat = _s5_output(ys_lat[0] + ys_lat[1], u_lat, d, w_glu, h_lat.dtype)
    out_ctx = _s5_output(ys_ctx[0] + ys_ctx[1], u_ctx, d, w_glu, h_ctx.dtype) if ctx_out else None
    return out_lat, out_ctx


def _axial_rope_tables(n_tokens):
    f32 = jnp.float32
    rows = n_tokens // GRID_W
    row = jnp.repeat(jnp.arange(rows, dtype=f32), GRID_W)
    col = jnp.tile(jnp.arange(GRID_W, dtype=f32), rows)
    half = DIFF_HEAD_DIM // 2
    inv = jnp.power(ROPE_BASE, -jnp.arange(0, half, 2, dtype=f32) / half)
    ang_r = row[:, None] * inv
    ang_c = col[:, None] * inv
    ex = lambda a: a[:, None, None, :]
    return (ex(jnp.cos(ang_r)), ex(jnp.sin(ang_r)), ex(jnp.cos(ang_c)), ex(jnp.sin(ang_c)))


def _rot_half(xh, cos, sin):
    x1, x2 = jnp.split(xh, 2, axis=-1)
    return jnp.concatenate([x1 * cos - x2 * sin, x1 * sin + x2 * cos], axis=-1)


def _axial_rope(x, rope):
    cr, sr, cc, sc = rope
    half = DIFF_HEAD_DIM // 2
    xf = x.astype(jnp.float32)
    out = jnp.concatenate([_rot_half(xf[..., :half], cr, sr), _rot_half(xf[..., half:], cc, sc)], axis=-1)
    return out.astype(x.dtype)


def _diff_qkv(h, w_qkv, q_norm, k_norm):
    bsz, n, _ = h.shape
    q, k, v = jnp.split(h @ w_qkv, 3, axis=-1)
    q = _rmsnorm(q.reshape(bsz, n, DIFF_HEADS, 2, DIFF_HEAD_DIM), q_norm)
    k = _rmsnorm(k.reshape(bsz, n, DIFF_HEADS, 2, DIFF_HEAD_DIM), k_norm)
    v = v.reshape(bsz, n, DIFF_HEADS, DIFF_V_DIM)
    return q, k, v


def _diff_attend(q, k, v, lam):
    s = jnp.einsum('bqhcd,bkhcd->bhcqk', q, k, preferred_element_type=jnp.float32) * (DIFF_HEAD_DIM ** -0.5)
    p = jax.nn.softmax(s, axis=-1)
    w = p[:, :, 0] - lam * p[:, :, 1]
    return jnp.einsum('bhqk,bkhe->bqhe', w.astype(v.dtype), v)


def _diff_mixer(h_lat, h_ctx, rope, w_qkv, q_norm, k_norm, lam_params, subln, w_o, lam_init, ctx_out):
    bsz, n_lat, _ = h_lat.shape
    q_l, k_l, v_l = _diff_qkv(h_lat, w_qkv, q_norm, k_norm)
    q_c, k_c, v_c = _diff_qkv(h_ctx, w_qkv, q_norm, k_norm)
    q_l = _axial_rope(q_l, rope)
    k_l = _axial_rope(k_l, rope)
    lp = lam_params.astype(jnp.float32)
    lam = jnp.exp(jnp.sum(lp[0] * lp[1])) - jnp.exp(jnp.sum(lp[2] * lp[3])) + lam_init
    k_all = jnp.concatenate([k_l, k_c], axis=1)
    v_all = jnp.concatenate([v_l, v_c], axis=1)
    n_blocks = n_lat // Q_BLOCK
    qb = q_l.reshape(bsz, n_blocks, Q_BLOCK, DIFF_HEADS, 2, DIFF_HEAD_DIM).transpose(1, 0, 2, 3, 4, 5)
    o = lax.map(lambda qq: _diff_attend(qq, k_all, v_all, lam), qb)
    o_l = o.transpose(1, 0, 2, 3, 4).reshape(bsz, n_lat, DIFF_HEADS, DIFF_V_DIM)

    def finish(oo):
        oo = _rmsnorm(oo, subln) * (1.0 - lam_init)
        return oo.reshape(oo.shape[0], oo.shape[1], D_MODEL) @ w_o

    out_l = finish(o_l)
    out_c = finish(_diff_attend(q_c, k_c, v_c, lam)) if ctx_out else None
    return out_l, out_c


def _fourier_mixer(h, w_f, b_f):
    bsz, n, _ = h.shape
    hg = h.astype(jnp.float32).reshape(bsz, n, FOURIER_GROUPS, FOURIER_CH)
    f = jnp.fft.fft2(hg, axes=(1, 3), norm='ortho').real
    return f.reshape(bsz, n, D_MODEL).astype(h.dtype) @ w_f + b_f


def setup_inputs(seed: int = 0) -> dict:
    key = jax.random.key(seed)
    ks = jax.random.split(key, 32)
    f32 = jnp.float32
    D = D_MODEL
    G, P, H = S5_GROUPS, S5_STATE, S5_GROUP
    nA, nB, nC = N_S5_LAYERS, N_DIFF_LAYERS, N_FOURIER_LAYERS

    def nrm(k, shape, std):
        return std * jax.random.normal(k, shape, f32)

    n_idx = jnp.arange(P, dtype=f32)
    return {
        'x': nrm(ks[0], (BATCH, SEQ, D), 1.0),
        'c': nrm(ks[1], (BATCH, D), 1.0),
        'ctx': nrm(ks[2], (BATCH, CTX_LEN, D), 1.0),
        'c_ctx': nrm(ks[3], (D,), 1.0),
        'w_mod': nrm(ks[4], (DEPTH, D, N_MOD * D), 0.5 * D ** -0.5),
        'b_mod': nrm(ks[5], (DEPTH, N_MOD * D), 0.02),
        'norm_g': 1.0 + nrm(ks[6], (DEPTH, 2, D), 0.02),
        'mlp_w1': nrm(ks[7], (DEPTH, D, D_FF), D ** -0.5),
        'mlp_w2': nrm(ks[8], (DEPTH, D_FF, D), D_FF ** -0.5),
        's5_lambda_re': -0.5 + nrm(ks[9], (nA, 2, G, P), 0.01),
        's5_lambda_im': math.pi * n_idx + nrm(ks[10], (nA, 2, G, P), 0.01),
        's5_log_dt': jax.random.uniform(ks[11], (nA, 2, G), f32, math.log(S5_DT_MIN), math.log(S5_DT_MAX)),
        's5_b_re': nrm(ks[12], (nA, 2, G, P, H), (2.0 * H) ** -0.5),
        's5_b_im': nrm(ks[13], (nA, 2, G, P, H), (2.0 * H) ** -0.5),
        's5_c_re': nrm(ks[14], (nA, 2, G, H, P), (2.0 * P) ** -0.5),
        's5_c_im': nrm(ks[15], (nA, 2, G, H, P), (2.0 * P) ** -0.5),
        's5_d': nrm(ks[16], (nA, D), 1.0),
        's5_w_glu': nrm(ks[17], (nA, D, 2 * D), D ** -0.5),
        'diff_w_qkv': nrm(ks[18], (nB, D, 3 * D), D ** -0.5),
        'diff_q_norm': 1.0 + nrm(ks[19], (nB, DIFF_HEAD_DIM), 0.02),
        'diff_k_norm': 1.0 + nrm(ks[20], (nB, DIFF_HEAD_DIM), 0.02),
        'diff_lambda': nrm(ks[21], (nB, 4, DIFF_HEAD_DIM), 0.1),
        'diff_subln': 1.0 + nrm(ks[22], (nB, DIFF_V_DIM), 0.02),
        'diff_w_o': nrm(ks[23], (nB, D, D), D ** -0.5),
        'fourier_w': nrm(ks[24], (nC, D, D), D ** -0.5),
        'fourier_b': nrm(ks[25], (nC, D), 0.02),
    }


def reference(x, c, ctx, c_ctx, w_mod, b_mod, norm_g, mlp_w1, mlp_w2,
              s5_lambda_re, s5_lambda_im, s5_log_dt, s5_b_re, s5_b_im, s5_c_re, s5_c_im,
              s5_d, s5_w_glu, diff_w_qkv, diff_q_norm, diff_k_norm, diff_lambda, diff_subln,
              diff_w_o, fourier_w, fourier_b):
    n_lat = x.shape[1]
    rope = _axial_rope_tables(n_lat)
    cond_lat = jax.nn.silu(c)
    cond_ctx = jax.nn.silu(c_ctx)
    for i in range(DEPTH):
        last = i == DEPTH - 1
        mod_l = (cond_lat @ w_mod[i] + b_mod[i]).reshape(-1, N_MOD, 1, D_MODEL)
        mod_c = (cond_ctx @ w_mod[i] + b_mod[i]).reshape(N_MOD, D_MODEL)
        h_l = _modulate(_rmsnorm(x, norm_g[i, 0]), mod_l[:, 0], mod_l[:, 1])
        h_c = _modulate(_rmsnorm(ctx, norm_g[i, 0]), mod_c[0], mod_c[1])
        kind = i % N_MIXERS
        j = i // N_MIXERS
        if kind == 0:
            o_l, o_c = _s5_mixer(h_l, h_c, s5_lambda_re[j], s5_lambda_im[j], s5_log_dt[j],
                                 s5_b_re[j], s5_b_im[j], s5_c_re[j], s5_c_im[j], s5_d[j],
                                 s5_w_glu[j], not last)
        elif kind == 1:
            lam_init = 0.8 - 0.6 * math.exp(-0.3 * i)
            o_l, o_c = _diff_mixer(h_l, h_c, rope, diff_w_qkv[j], diff_q_norm[j], diff_k_norm[j],
                                   diff_lambda[j], diff_subln[j], diff_w_o[j], lam_init, not last)
        else:
            o_l = _fourier_mixer(h_l, fourier_w[j], fourier_b[j])
            o_c = None if last else _fourier_mixer(h_c, fourier_w[j], fourier_b[j])
        x = x + mod_l[:, 2] * o_l
        x = x + mod_l[:, 5] * _sqrelu_mlp(
            _modulate(_rmsnorm(x, norm_g[i, 1]), mod_l[:, 3], mod_l[:, 4]), mlp_w1[i], mlp_w2[i])
        if not last:
            ctx = ctx + mod_c[2] * o_c
            ctx = ctx + mod_c[5] * _sqrelu_mlp(
                _modulate(_rmsnorm(ctx, norm_g[i, 1]), mod_c[3], mod_c[4]), mlp_w1[i], mlp_w2[i])
    return x
```

```python
import functools
import math

import numpy as np
import jax
import jax.numpy as jnp
from jax import lax
from jax.experimental import pallas as pl
from jax.experimental.pallas import tpu as pltpu

F32 = jnp.float32
BF16 = jnp.bfloat16

D_MODEL = 1024
N_MOD = 6
NORM_EPS = 1e-6
GRID_W = 64
S5_GROUP = 16
S5_GROUPS = D_MODEL // S5_GROUP
S5_STATE = 64
S5_CHUNK = 16
DIFF_HEAD_DIM = 64
DIFF_HEADS = D_MODEL // (2 * DIFF_HEAD_DIM)
DIFF_V_DIM = 2 * DIFF_HEAD_DIM
ROPE_BASE = 10000.0
FOURIER_GROUPS = 4
FOURIER_CH = D_MODEL // FOURIER_GROUPS

LANES = 128
VMEM_LIMIT_BYTES = 48 * 1024 * 1024


def _pick(n, pref):
    t = min(n, pref)
    while n % t:
        t //= 2
    return t


def _params(*sem):
    return pltpu.CompilerParams(dimension_semantics=sem, vmem_limit_bytes=VMEM_LIMIT_BYTES)


def _norm_mod(x, g, shift, scale):
    ms = jnp.mean(x * x, axis=-1, keepdims=True)
    return (x * lax.rsqrt(ms + NORM_EPS) * g) * (1.0 + scale) + shift


def _mod_spec(mod, rows_per_batch, tm):
    if mod.shape[0] == 1:
        return pl.BlockSpec((1, N_MOD, D_MODEL), lambda i, *_: (0, 0, 0))
    return pl.BlockSpec((1, N_MOD, D_MODEL), lambda i, *_: ((i * tm) // rows_per_batch, 0, 0))


def _mod_kernel(c_ref, w_ref, b_ref, o_ref):
    c = c_ref[...]
    s = c * jax.nn.sigmoid(c)
    o_ref[0] = jnp.dot(s.astype(BF16), w_ref[0].astype(BF16), preferred_element_type=F32) + b_ref[0]


def _modulation(cond, w_mod, b_mod):
    depth, _, n_out = w_mod.shape
    r = cond.shape[0]
    tn = _pick(n_out, 1536)
    return pl.pallas_call(
        _mod_kernel,
        grid=(depth, n_out // tn),
        in_specs=[pl.BlockSpec((r, D_MODEL), lambda i, j: (0, 0)),
                  pl.BlockSpec((1, D_MODEL, tn), lambda i, j: (i, 0, j)),
                  pl.BlockSpec((1, 1, tn), lambda i, j: (i, 0, j))],
        out_specs=pl.BlockSpec((1, r, tn), lambda i, j: (i, 0, j)),
        out_shape=jax.ShapeDtypeStruct((depth, r, n_out), F32),
        compiler_params=_params("parallel", "parallel"),
        name="adaln_modulation",
    )(cond, w_mod, b_mod.reshape(depth, 1, n_out))


def _mlp_kernel(x_ref, g_ref, mod_ref, w1_ref, w2_ref, o_ref, h_scr, acc_scr):
    j = pl.program_id(1)

    @pl.when(j == 0)
    def _():
        h = _norm_mod(x_ref[...], g_ref[...], mod_ref[0, 3:4, :], mod_ref[0, 4:5, :])
        h_scr[...] = h.astype(BF16)
        acc_scr[...] = jnp.zeros_like(acc_scr)

    a = jnp.dot(h_scr[...], w1_ref[...], preferred_element_type=F32)
    a = jnp.maximum(a, 0.0)
    a = a * a
    acc_scr[...] += jnp.dot(a.astype(BF16), w2_ref[...], preferred_element_type=F32)

    @pl.when(j == pl.num_programs(1) - 1)
    def _():
        o_ref[...] = x_ref[...] + mod_ref[0, 5:6, :] * acc_scr[...]


def _mlp(x, g, mod, w1, w2, rows_per_batch):
    m = x.shape[0]
    d_ff = w1.shape[1]
    tm = _pick(rows_per_batch, 1024)
    tf = _pick(d_ff, 512)
    return pl.pallas_call(
        _mlp_kernel,
        grid=(m // tm, d_ff // tf),
        in_specs=[pl.BlockSpec((tm, D_MODEL), lambda i, j: (i, 0)),
                  pl.BlockSpec((1, D_MODEL), lambda i, j: (0, 0)),
                  _mod_spec(mod, rows_per_batch, tm),
                  pl.BlockSpec((D_MODEL, tf), lambda i, j: (0, j)),
                  pl.BlockSpec((tf, D_MODEL), lambda i, j: (j, 0))],
        out_specs=pl.BlockSpec((tm, D_MODEL), lambda i, j: (i, 0)),
        out_shape=jax.ShapeDtypeStruct((m, D_MODEL), F32),
        scratch_shapes=[pltpu.VMEM((tm, D_MODEL), BF16), pltpu.VMEM((tm, D_MODEL), F32)],
        compiler_params=_params("parallel", "arbitrary"),
        name="sqrelu_mlp",
    )(x, g, mod, w1, w2)


def _linear_residual_kernel(a_ref, w_ref, b_ref, x_ref, mod_ref, o_ref, *, gate_row):
    y = jnp.dot(a_ref[...], w_ref[...], preferred_element_type=F32) + b_ref[...]
    o_ref[...] = x_ref[...] + mod_ref[0, gate_row:gate_row + 1, :] * y


def _linear_residual(a, w, b, x, mod, rows_per_batch, gate_row):
    m, k = a.shape
    tm = _pick(rows_per_batch, 512)
    return pl.pallas_call(
        functools.partial(_linear_residual_kernel, gate_row=gate_row),
        grid=(m // tm,),
        in_specs=[pl.BlockSpec((tm, k), lambda i: (i, 0)),
                  pl.BlockSpec((k, D_MODEL), lambda i: (0, 0)),
                  pl.BlockSpec((1, D_MODEL), lambda i: (0, 0)),
                  pl.BlockSpec((tm, D_MODEL), lambda i: (i, 0)),
                  _mod_spec(mod, rows_per_batch, tm)],
        out_specs=pl.BlockSpec((tm, D_MODEL), lambda i: (i, 0)),
        out_shape=jax.ShapeDtypeStruct((m, D_MODEL), F32),
        compiler_params=_params("parallel"),
        name="linear_residual",
    )(a, w, b, x, mod)


def _norm_mod_kernel(x_ref, g_ref, mod_ref, o_ref):
    h = _norm_mod(x_ref[...], g_ref[...], mod_ref[0, 0:1, :], mod_ref[0, 1:2, :])
    o_ref[...] = h.astype(o_ref.dtype)


def _norm_mod_call(x, g, mod, rows_per_batch):
    m = x.shape[0]
    tm = _pick(rows_per_batch, 1024)
    return pl.pallas_call(
        _norm_mod_kernel,
        grid=(m // tm,),
        in_specs=[pl.BlockSpec((tm, D_MODEL), lambda i: (i, 0)),
                  pl.BlockSpec((1, D_MODEL), lambda i: (0, 0)),
                  _mod_spec(mod, rows_per_batch, tm)],
        out_specs=pl.BlockSpec((tm, D_MODEL), lambda i: (i, 0)),
        out_shape=jax.ShapeDtypeStruct((m, D_MODEL), BF16),
        compiler_params=_params("parallel"),
        name="s5_norm_mod",
    )(x, g, mod)


def _s5_core_kernel(u_ref, m_ref, we_ref, ws_ref, a_ref, y_ref,
                    er, ei, sfr, sfi, sbr, sbi, *, n_ctx_chunks, n_chunks, batch):
    half = LANES // 2
    u = u_ref[0]
    e = jnp.dot(u, we_ref[0], preferred_element_type=F32)
    er[...] = e[:, :LANES].reshape(n_chunks, batch, LANES)
    ei[...] = e[:, LANES:].reshape(n_chunks, batch, LANES)
    ar = jnp.broadcast_to(a_ref[0, 0:1, :], (batch, LANES))
    ai = jnp.broadcast_to(a_ref[0, 1:2, :], (batch, LANES))
    is_fwd = lax.broadcasted_iota(jnp.int32, (batch, LANES), 1) < half

    def scan_segment(lo, hi, carry):
        def step(k, c):
            sr, si = c
            kb = lo + hi - 1 - k
            sfr[k] = sr
            sfi[k] = si
            sbr[kb] = sr
            sbi[kb] = si
            e_r = jnp.where(is_fwd, er[k], er[kb])
            e_i = jnp.where(is_fwd, ei[k], ei[kb])
            return (ar * sr - ai * si + e_r, ar * si + ai * sr + e_i)
        return lax.fori_loop(lo, hi, step, carry)

    zero = jnp.zeros((batch, LANES), F32)
    carry = scan_segment(0, n_ctx_chunks, (zero, zero))
    scan_segment(n_ctx_chunks, n_chunks, carry)

    is_fwd3 = lax.broadcasted_iota(jnp.int32, (n_chunks, batch, LANES), 2) < half
    s_re = jnp.where(is_fwd3, sfr[...], sbr[...]).reshape(n_chunks * batch, LANES)
    s_im = jnp.where(is_fwd3, sfi[...], sbi[...]).reshape(n_chunks * batch, LANES)
    s_cat = jnp.concatenate([s_re, s_im], axis=-1).astype(BF16)
    y_ref[0] = (jnp.dot(u, m_ref[0], preferred_element_type=F32)
                + jnp.dot(s_cat, ws_ref[0], preferred_element_type=F32))


def _s5_core(u, m_op, we_op, ws_op, a_op, n_ctx_chunks, n_chunks, batch):
    g, r, k = u.shape
    scr = pltpu.VMEM((n_chunks, batch, LANES), F32)
    return pl.pallas_call(
        functools.partial(_s5_core_kernel, n_ctx_chunks=n_ctx_chunks, n_chunks=n_chunks, batch=batch),
        grid=(g,),
        in_specs=[pl.BlockSpec((1, r, k), lambda i: (i, 0, 0)),
                  pl.BlockSpec((1, k, k), lambda i: (i, 0, 0)),
                  pl.BlockSpec((1, k, k), lambda i: (i, 0, 0)),
                  pl.BlockSpec((1, k, k), lambda i: (i, 0, 0)),
                  pl.BlockSpec((1, 2, LANES), lambda i: (i, 0, 0))],
        out_specs=pl.BlockSpec((1, r, k), lambda i: (i, 0, 0)),
        out_shape=jax.ShapeDtypeStruct((g, r, k), F32),
        scratch_shapes=[scr] * 6,
        compiler_params=_params("parallel"),
        name="s5_chunk_scan",
    )(u, m_op, we_op, ws_op, a_op)


def _s5_out_kernel(x_ref, y_ref, g_ref, mod_ref, d_ref, w_ref, o_ref):
    x = x_ref[...]
    h = _norm_mod(x, g_ref[...], mod_ref[0, 0:1, :], mod_ref[0, 1:2, :])
    z = jax.nn.gelu(y_ref[...] + d_ref[...] * h, approximate=True)
    gg = jnp.dot(z.astype(BF16), w_ref[...], preferred_element_type=F32)
    o = gg[:, :D_MODEL] * jax.nn.sigmoid(gg[:, D_MODEL:])
    o_ref[...] = x + mod_ref[0, 2:3, :] * o


def _s5_out(x, y, g, mod, d, w_glu, rows_per_batch):
    m = x.shape[0]
    tm = _pick(rows_per_batch, 512)
    return pl.pallas_call(
        _s5_out_kernel,
        grid=(m // tm,),
        in_specs=[pl.BlockSpec((tm, D_MODEL), lambda i: (i, 0)),
                  pl.BlockSpec((tm, D_MODEL), lambda i: (i, 0)),
                  pl.BlockSpec((1, D_MODEL), lambda i: (0, 0)),
                  _mod_spec(mod, rows_per_batch, tm),
                  pl.BlockSpec((1, D_MODEL), lambda i: (0, 0)),
                  pl.BlockSpec((D_MODEL, 2 * D_MODEL), lambda i: (0, 0))],
        out_specs=pl.BlockSpec((tm, D_MODEL), lambda i: (i, 0)),
        out_shape=jax.ShapeDtypeStruct((m, D_MODEL), F32),
        compiler_params=_params("parallel"),
        name="s5_glu_out",
    )(x, y, g, mod, d, w_glu)


def _s5_operators(lam_re, lam_im, log_dt, b_re, b_im, c_re, c_im):
    hp = lax.Precision.HIGHEST
    t_len = S5_CHUNK
    lam_re = lam_re.astype(F32)
    lam_im = lam_im.astype(F32)
    dt = jnp.exp(log_dt.astype(F32))[..., None]
    mag = jnp.exp(lam_re * dt)
    a_re = mag * jnp.cos(lam_im * dt)
    a_im = mag * jnp.sin(lam_im * dt)
    n_re = a_re - 1.0
    n_im = a_im
    den = lam_re * lam_re + lam_im * lam_im
    k_re = (n_re * lam_re + n_im * lam_im) / den
    k_im = (n_im * lam_re - n_re * lam_im) / den
    b_re = b_re.astype(F32)
    b_im = b_im.astype(F32)
    bb_re = k_re[..., None] * b_re - k_im[..., None] * b_im
    bb_im = k_re[..., None] * b_im + k_im[..., None] * b_re
    c_re = c_re.astype(F32)
    c_im = c_im.astype(F32)

    pr = [jnp.ones_like(a_re)]
    pi = [jnp.zeros_like(a_im)]
    for _ in range(t_len):
        pr.append(pr[-1] * a_re - pi[-1] * a_im)
        pi.append(pr[-2] * a_im + pi[-1] * a_re)
    pr = jnp.stack(pr)
    pi = jnp.stack(pi)

    car = c_re[None] * pr[:, :, :, None, :] - c_im[None] * pi[:, :, :, None, :]
    cai = c_re[None] * pi[:, :, :, None, :] + c_im[None] * pr[:, :, :, None, :]
    kern = (jnp.einsum('ldgop,dgph->ldgoh', car, bb_re, precision=hp)
            - jnp.einsum('ldgop,dgph->ldgoh', cai, bb_im, precision=hp))

    s_idx = np.arange(t_len)[:, None]
    t_idx = np.arange(t_len)[None, :]
    lag_f = np.clip(t_idx - s_idx, 0, t_len)
    lag_b = np.clip(s_idx - t_idx, 0, t_len)
    mask_f = jnp.asarray((s_idx <= t_idx).astype(np.float32))[:, :, None, None, None]
    mask_b = jnp.asarray((s_idx >= t_idx).astype(np.float32))[:, :, None, None, None]
    mm = kern[lag_f, 0] * mask_f + kern[lag_b, 1] * mask_b
    g = mm.shape[2]
    th = t_len * S5_GROUP
    m_op = mm.transpose(2, 0, 4, 1, 3).reshape(g, th, th)

    abr = pr[:t_len, :, :, :, None] * bb_re[None] - pi[:t_len, :, :, :, None] * bb_im[None]
    abi = pr[:t_len, :, :, :, None] * bb_im[None] + pi[:t_len, :, :, :, None] * bb_re[None]

    def to_rows(z):
        return z.transpose(1, 0, 3, 2).reshape(g, th, S5_STATE)
    we_op = jnp.concatenate([to_rows(abr[::-1, 0]), to_rows(abr[:, 1]),
                             to_rows(abi[::-1, 0]), to_rows(abi[:, 1])], axis=-1)

    def to_cols(z):
        return z.transpose(1, 3, 0, 2).reshape(g, S5_STATE, th)
    ws_op = jnp.concatenate([to_cols(car[1:, 0]), to_cols(car[1:, 1][::-1]),
                             to_cols(-cai[1:, 0]), to_cols(-cai[1:, 1][::-1])], axis=1)

    a_op = jnp.stack([jnp.concatenate([pr[t_len, 0], pr[t_len, 1]], axis=-1),
                      jnp.concatenate([pi[t_len, 0], pi[t_len, 1]], axis=-1)], axis=1)
    return m_op.astype(BF16), we_op.astype(BF16), ws_op.astype(BF16), a_op


def _s5_layer(x_l, x_c, g, mod_l, mod_c, ops, d, w_glu, bsz, n_lat, n_ctx, ctx_out):
    h_l = _norm_mod_call(x_l, g, mod_l, n_lat).reshape(bsz, n_lat, D_MODEL)
    h_c = _norm_mod_call(x_c, g, mod_c, n_ctx).reshape(bsz, n_ctx, D_MODEL)
    n_tok = n_ctx + n_lat
    n_chunks = n_tok // S5_CHUNK
    h_all = jnp.concatenate([h_c, h_l], axis=1)
    u = h_all.reshape(bsz, n_chunks, S5_CHUNK, S5_GROUPS, S5_GROUP)
    u = u.transpose(3, 1, 0, 2, 4).reshape(S5_GROUPS, n_chunks * bsz, S5_CHUNK * S5_GROUP)
    y = _s5_core(u, *ops, n_ctx // S5_CHUNK, n_chunks, bsz)
    y = y.reshape(S5_GROUPS, n_chunks, bsz, S5_CHUNK, S5_GROUP)
    y = y.transpose(2, 1, 3, 0, 4).reshape(bsz, n_tok, D_MODEL)
    y_l = y[:, n_ctx:].reshape(bsz * n_lat, D_MODEL)
    x_l = _s5_out(x_l, y_l, g, mod_l, d, w_glu, n_lat)
    if ctx_out:
        y_c = y[:, :n_ctx].reshape(bsz * n_ctx, D_MODEL)
        x_c = _s5_out(x_c, y_c, g, mod_c, d, w_glu, n_ctx)
    return x_l, x_c


def _swap16(t):
    lane = lax.broadcasted_iota(jnp.int32, t.shape, 1)
    first = (lane % 32) < 16
    return jnp.where(first, pltpu.roll(t, LANES - 16, 1), pltpu.roll(t, 16, 1))


def _qkv_kernel(x_ref, g_ref, mod_ref, w_ref, qn_ref, kn_ref, cos_ref, sin_ref, o_ref, h_scr, *, rope):
    j = pl.program_id(1)

    @pl.when(j == 0)
    def _():
        h = _norm_mod(x_ref[...], g_ref[...], mod_ref[0, 0:1, :], mod_ref[0, 1:2, :])
        h_scr[...] = h.astype(BF16)

    y = jnp.dot(h_scr[...], w_ref[...], preferred_element_type=F32)

    @pl.when(j == 2)
    def _():
        o_ref[0] = y.astype(BF16)

    @pl.when(j < 2)
    def _():
        is_q = j == 0
        gn = jnp.where(is_q, qn_ref[...], kn_ref[...])
        out_scale = jnp.where(is_q, DIFF_HEAD_DIM ** -0.5, 1.0)
        tm = y.shape[0]
        lo = lax.broadcasted_iota(jnp.int32, (tm, LANES), 1) < DIFF_HEAD_DIM
        for hh in range(DIFF_HEADS):
            t = y[:, hh * LANES:(hh + 1) * LANES]
            t2 = t * t
            s0 = jnp.sum(jnp.where(lo, t2, 0.0), axis=-1, keepdims=True)
            s1 = jnp.sum(jnp.where(lo, 0.0, t2), axis=-1, keepdims=True)
            r = jnp.where(lo, lax.rsqrt(s0 / DIFF_HEAD_DIM + NORM_EPS), lax.rsqrt(s1 / DIFF_HEAD_DIM + NORM_EPS))
            t = t * r * gn
            if rope:
                t = t * cos_ref[...] + _swap16(t) * sin_ref[...]
            o_ref[0, :, hh * LANES:(hh + 1) * LANES] = (t * out_scale).astype(BF16)


def _qkv(x, g, mod, w_qkv, qn, kn, cos_t, sin_t, rows_per_batch, rope):
    m = x.shape[0]
    tm = _pick(rows_per_batch, 512)
    nblk = rows_per_batch // tm
    return pl.pallas_call(
        functools.partial(_qkv_kernel, rope=rope),
        grid=(m // tm, 3),
        in_specs=[pl.BlockSpec((tm, D_MODEL), lambda i, j: (i, 0)),
                  pl.BlockSpec((1, D_MODEL), lambda i, j: (0, 0)),
                  _mod_spec(mod, rows_per_batch, tm),
                  pl.BlockSpec((D_MODEL, D_MODEL), lambda i, j: (0, j)),
                  pl.BlockSpec((1, LANES), lambda i, j: (0, 0)),
                  pl.BlockSpec((1, LANES), lambda i, j: (0, 0)),
                  pl.BlockSpec((tm, LANES), lambda i, j: (i % nblk, 0)),
                  pl.BlockSpec((tm, LANES), lambda i, j: (i % nblk, 0))],
        out_specs=pl.BlockSpec((1, tm, D_MODEL), lambda i, j: (j, i, 0)),
        out_shape=jax.ShapeDtypeStruct((3, m, D_MODEL), BF16),
        scratch_shapes=[pltpu.VMEM((tm, D_MODEL), BF16)],
        compiler_params=_params("parallel", "arbitrary"),
        name="diff_qkv",
    )(x, g, mod, w_qkv, qn, kn, cos_t, sin_t)


def _flash_kernel(q_ref, k_ref, v_ref, lam_ref, sub_ref, o_ref, *, tk, lam_init):
    tq = q_ref.shape[1]
    nk = k_ref.shape[1]
    q = q_ref[0]
    lane = lax.broadcasted_iota(jnp.int32, q.shape, 1)
    zero = jnp.zeros_like(q)
    qq = jnp.concatenate([jnp.where(lane < DIFF_HEAD_DIM, q, zero),
                          jnp.where(lane < DIFF_HEAD_DIM, zero, q)], axis=0)

    def body(j, carry):
        m, l, acc = carry
        off = pl.multiple_of(j * tk, tk)
        kk = k_ref[0, pl.ds(off, tk), :]
        vv = v_ref[0, pl.ds(off, tk), :]
        s = lax.dot_general(qq, kk, (((1,), (1,)), ((), ())), preferred_element_type=F32)
        m_new = jnp.maximum(m, jnp.max(s, axis=-1, keepdims=True))
        alpha = jnp.exp(m - m_new)
        p = jnp.exp(s - m_new)
        l = alpha * l + jnp.sum(p, axis=-1, keepdims=True)
        acc = alpha * acc + jnp.dot(p.astype(BF16), vv, preferred_element_type=F32)
        return m_new, l, acc

    init = (jnp.full((2 * tq, 1), -jnp.inf, F32), jnp.zeros((2 * tq, 1), F32),
            jnp.zeros((2 * tq, DIFF_V_DIM), F32))
    _, l, acc = lax.fori_loop(0, nk // tk, body, init)

    lp = lam_ref[...]
    lam = (jnp.exp(jnp.sum(lp[0:1] * lp[1:2], axis=-1, keepdims=True))
           - jnp.exp(jnp.sum(lp[2:3] * lp[3:4], axis=-1, keepdims=True)) + lam_init)
    o = acc[:tq] / l[:tq] - lam * (acc[tq:] / l[tq:])
    ms = jnp.mean(o * o, axis=-1, keepdims=True)
    o = o * lax.rsqrt(ms + NORM_EPS) * sub_ref[...] * (1.0 - lam_init)
    o_ref[0] = o.astype(o_ref.dtype)


def _flash(q, k, v, lam_params, subln, lam_init):
    bsz, nq, _ = q.shape
    nk = k.shape[1]
    tq = _pick(nq, 256)
    tk = _pick(nk, 256)
    return pl.pallas_call(
        functools.partial(_flash_kernel, tk=tk, lam_init=lam_init),
        grid=(bsz, DIFF_HEADS, nq // tq),
        in_specs=[pl.BlockSpec((1, tq, LANES), lambda b, h, i: (b, i, h)),
                  pl.BlockSpec((1, nk, LANES), lambda b, h, i: (b, 0, h)),
                  pl.BlockSpec((1, nk, LANES), lambda b, h, i: (b, 0, h)),
                  pl.BlockSpec((4, DIFF_HEAD_DIM), lambda b, h, i: (0, 0)),
                  pl.BlockSpec((1, DIFF_V_DIM), lambda b, h, i: (0, 0))],
        out_specs=pl.BlockSpec((1, tq, LANES), lambda b, h, i: (b, i, h)),
        out_shape=jax.ShapeDtypeStruct((bsz, nq, D_MODEL), BF16),
        compiler_params=_params("parallel", "parallel", "arbitrary"),
        name="diff_flash_attention",
    )(q, k, v, lam_params, subln)


def _rope_tables(n_tokens):
    rows = n_tokens // GRID_W
    row = jnp.repeat(jnp.arange(rows, dtype=F32), GRID_W)
    col = jnp.tile(jnp.arange(GRID_W, dtype=F32), rows)
    half = DIFF_HEAD_DIM // 2
    inv = jnp.power(ROPE_BASE, -jnp.arange(0, half, 2, dtype=F32) / half)
    ang_r = row[:, None] * inv
    ang_c = col[:, None] * inv
    cos64 = jnp.concatenate([jnp.cos(ang_r), jnp.cos(ang_r), jnp.cos(ang_c), jnp.cos(ang_c)], axis=-1)
    sin64 = jnp.concatenate([-jnp.sin(ang_r), jnp.sin(ang_r), -jnp.sin(ang_c), jnp.sin(ang_c)], axis=-1)
    return jnp.tile(cos64, (1, 2)), jnp.tile(sin64, (1, 2))


def _diff_layer(x_l, x_c, g, mod_l, mod_c, w_qkv, q_norm, k_norm, lam_params, subln, w_o,
                lam_init, bsz, n_lat, n_ctx, ctx_out):
    cos_t, sin_t = _rope_tables(n_lat)
    qn = jnp.tile(q_norm.astype(F32), 2)[None]
    kn = jnp.tile(k_norm.astype(F32), 2)[None]
    w = w_qkv.astype(BF16)
    qkv_l = _qkv(x_l, g, mod_l, w, qn, kn, cos_t, sin_t, n_lat, True).reshape(3, bsz, n_lat, D_MODEL)
    qkv_c = _qkv(x_c, g, mod_c, w, qn, kn, cos_t, sin_t, n_ctx, False).reshape(3, bsz, n_ctx, D_MODEL)
    k_all = jnp.concatenate([qkv_l[1], qkv_c[1]], axis=1)
    v_all = jnp.concatenate([qkv_l[2], qkv_c[2]], axis=1)
    lam_p = lam_params.astype(F32)
    sub = subln.astype(F32)[None]
    wo = w_o.astype(BF16)
    zero_b = jnp.zeros((1, D_MODEL), F32)
    o_l = _flash(qkv_l[0], k_all, v_all, lam_p, sub, lam_init).reshape(bsz * n_lat, D_MODEL)
    x_l = _linear_residual(o_l, wo, zero_b, x_l, mod_l, n_lat, 2)
    if ctx_out:
        o_c = _flash(qkv_c[0], qkv_c[1], qkv_c[2], lam_p, sub, lam_init).reshape(bsz * n_ctx, D_MODEL)
        x_c = _linear_residual(o_c, wo, zero_b, x_c, mod_c, n_ctx, 2)
    return x_l, x_c


def _fft1_kernel(x_ref, g_ref, mod_ref, t_ref, o_ref, *, n1, group):
    gamma = g_ref[...]
    shift = mod_ref[0, 0:1, :]
    scale = mod_ref[0, 1:2, :]
    for jj in range(group):
        sl = slice(jj * D_MODEL, (jj + 1) * D_MODEL)
        h = _norm_mod(x_ref[0, :, sl], gamma, shift, scale).astype(BF16)
        a = jnp.dot(t_ref[jj], h, preferred_element_type=F32)
        o_ref[0, 0, :, sl] = a[:n1].astype(BF16)
        o_ref[0, 1, :, sl] = a[n1:].astype(BF16)


def _fft2_kernel(a_ref, x_ref, mod_ref, f2_ref, cd_ref, w_ref, b_ref, o_ref, zr_scr, zi_scr, *, n2, group):
    for kk in range(group):
        a = jnp.concatenate([a_ref[0, 0, kk], a_ref[0, 1, kk]], axis=0)
        z = jnp.dot(f2_ref[...], a, preferred_element_type=F32)
        zr_scr[kk * n2:(kk + 1) * n2, :] = z[:n2].astype(BF16)
        zi_scr[kk * n2:(kk + 1) * n2, :] = z[n2:].astype(BF16)
    zr = zr_scr[...]
    zi = zi_scr[...]
    fs = []
    for cg in range(FOURIER_GROUPS):
        sl = slice(cg * FOURIER_CH, (cg + 1) * FOURIER_CH)
        fs.append(jnp.dot(zr[:, sl], cd_ref[0], preferred_element_type=F32)
                  + jnp.dot(zi[:, sl], cd_ref[1], preferred_element_type=F32))
    f = jnp.concatenate(fs, axis=-1).astype(BF16)
    o = jnp.dot(f, w_ref[...], preferred_element_type=F32) + b_ref[...]
    gate = mod_ref[0, 2:3, :]
    for kk in range(group):
        sl = slice(kk * D_MODEL, (kk + 1) * D_MODEL)
        o_ref[0, :, sl] = x_ref[0, :, sl] + gate * o[kk * n2:(kk + 1) * n2]


def _fourier_tables(n):
    n1 = int(round(math.sqrt(n)))
    n2 = n // n1
    assert n1 * n2 == n and n1 % 8 == 0 and n2 % 8 == 0
    k1 = np.arange(n1)[None, :, None]
    t1 = np.arange(n1)[None, None, :]
    t2 = np.arange(n2)[:, None, None]
    ang = 2.0 * np.pi * ((k1 * (n2 * t1 + t2)) % n) / n
    t_tab = np.concatenate([np.cos(ang), -np.sin(ang)], axis=1)
    k2 = np.arange(n2)[:, None]
    tt = np.arange(n2)[None, :]
    ang2 = 2.0 * np.pi * ((k2 * tt) % n2) / n2
    c2, s2 = np.cos(ang2), np.sin(ang2)
    f2 = np.block([[c2, s2], [-s2, c2]])
    cc = np.arange(FOURIER_CH)
    ang3 = 2.0 * np.pi * ((cc[:, None] * cc[None, :]) % FOURIER_CH) / FOURIER_CH
    norm = 1.0 / math.sqrt(n * FOURIER_CH)
    cd = np.stack([np.cos(ang3), np.sin(ang3)]) * norm
    as_bf16 = lambda z: jnp.asarray(z, dtype=F32).astype(BF16)
    return n1, n2, as_bf16(t_tab), as_bf16(f2), as_bf16(cd)


def _fourier_layer_one(x, g, mod, w_f, b_f, bsz, n):
    n1, n2, t_tab, f2, cd = _fourier_tables(n)
    g1 = _pick(n2, 8)
    g2 = _pick(n1, 8)
    batch_idx = (lambda b: b) if mod.shape[0] > 1 else (lambda b: 0)
    mod_spec = pl.BlockSpec((1, N_MOD, D_MODEL), lambda b, j: (batch_idx(b), 0, 0))
    a = pl.pallas_call(
        functools.partial(_fft1_kernel, n1=n1, group=g1),
        grid=(bsz, n2 // g1),
        in_specs=[pl.BlockSpec((1, n1, g1 * D_MODEL), lambda b, j: (b, 0, j)),
                  pl.BlockSpec((1, D_MODEL), lambda b, j: (0, 0)),
                  mod_spec,
                  pl.BlockSpec((g1, 2 * n1, n1), lambda b, j: (j, 0, 0))],
        out_specs=pl.BlockSpec((1, 2, n1, g1 * D_MODEL), lambda b, j: (b, 0, 0, j)),
        out_shape=jax.ShapeDtypeStruct((bsz, 2, n1, n2 * D_MODEL), BF16),
        compiler_params=_params("parallel", "parallel"),
        name="fourier_token_dft_stage1",
    )(x.reshape(bsz, n1, n2 * D_MODEL), g, mod, t_tab)
    a = a.reshape(bsz, 2, n1, n2, D_MODEL)
    out = pl.pallas_call(
        functools.partial(_fft2_kernel, n2=n2, group=g2),
        grid=(bsz, n1 // g2),
        in_specs=[pl.BlockSpec((1, 2, g2, n2, D_MODEL), lambda b, j: (b, 0, j, 0, 0)),
                  pl.BlockSpec((1, n2, g2 * D_MODEL), lambda b, j: (b, 0, j)),
                  mod_spec,
                  pl.BlockSpec((2 * n2, 2 * n2), lambda b, j: (0, 0)),
                  pl.BlockSpec((2, FOURIER_CH, FOURIER_CH), lambda b, j: (0, 0, 0)),
                  pl.BlockSpec((D_MODEL, D_MODEL), lambda b, j: (0, 0)),
                  pl.BlockSpec((1, D_MODEL), lambda b, j: (0, 0))],
        out_specs=pl.BlockSpec((1, n2, g2 * D_MODEL), lambda b, j: (b, 0, j)),
        out_shape=jax.ShapeDtypeStruct((bsz, n2, n1 * D_MODEL), F32),
        scratch_shapes=[pltpu.VMEM((g2 * n2, D_MODEL), BF16), pltpu.VMEM((g2 * n2, D_MODEL), BF16)],
        compiler_params=_params("parallel", "parallel"),
        name="fourier_token_dft_stage2",
    )(a, x.reshape(bsz, n2, n1 * D_MODEL), mod, f2, cd, w_f, b_f)
    return out.reshape(bsz * n, D_MODEL)


def kernel(x, c, ctx, c_ctx, w_mod, b_mod, norm_g, mlp_w1, mlp_w2, s5_lambda_re, s5_lambda_im, s5_log_dt, s5_b_re, s5_b_im, s5_c_re, s5_c_im, s5_d, s5_w_glu, diff_w_qkv, diff_q_norm, diff_k_norm, diff_lambda, diff_subln, diff_w_o, fourier_w, fourier_b):
    bsz, n_lat, _ = x.shape
    n_ctx = ctx.shape[1]
    depth = w_mod.shape[0]
    assert bsz == 8 and n_lat % GRID_W == 0 and n_ctx % S5_CHUNK == 0 and n_lat % S5_CHUNK == 0

    cond_rows = 16
    cond = jnp.concatenate([c.astype(F32), c_ctx.astype(F32)[None],
                            jnp.zeros((cond_rows - bsz - 1, D_MODEL), F32)], axis=0)
    mod_all = _modulation(cond, w_mod, b_mod).reshape(depth, cond_rows, N_MOD, D_MODEL)

    x_l = x.reshape(bsz * n_lat, D_MODEL)
    x_c = ctx.reshape(bsz * n_ctx, D_MODEL)
    for i in range(depth):
        last = i == depth - 1
        mod_l = mod_all[i, :bsz]
        mod_c = mod_all[i, bsz:bsz + 1]
        g_mix = norm_g[i, 0][None]
        g_mlp = norm_g[i, 1][None]
        kind = i % 3
        j = i // 3
        if kind == 0:
            ops = _s5_operators(s5_lambda_re[j], s5_lambda_im[j], s5_log_dt[j], s5_b_re[j], s5_b_im[j],
                                s5_c_re[j], s5_c_im[j])
            x_l, x_c = _s5_layer(x_l, x_c, g_mix, mod_l, mod_c, ops, s5_d[j][None],
                                 s5_w_glu[j].astype(BF16), bsz, n_lat, n_ctx, not last)
        elif kind == 1:
            lam_init = 0.8 - 0.6 * math.exp(-0.3 * i)
            x_l, x_c = _diff_layer(x_l, x_c, g_mix, mod_l, mod_c, diff_w_qkv[j], diff_q_norm[j],
                                   diff_k_norm[j], diff_lambda[j], diff_subln[j], diff_w_o[j],
                                   lam_init, bsz, n_lat, n_ctx, not last)
        else:
            w_f = fourier_w[j].astype(BF16)
            b_f = fourier_b[j][None]
            x_l = _fourier_layer_one(x_l, g_mix, mod_l, w_f, b_f, bsz, n_lat)
            if not last:
                x_c = _fourier_layer_one(x_c, g_mix, mod_c, w_f, b_f, bsz, n_ctx)
        w1 = mlp_w1[i].astype(BF16)
        w2 = mlp_w2[i].astype(BF16)
        x_l = _mlp(x_l, g_mlp, mod_l, w1, w2, n_lat)
        if not last:
            x_c = _mlp(x_c, g_mlp, mod_c, w1, w2, n_ctx)
    return x_l.reshape(bsz, n_lat, D_MODEL)
```

```python
import functools
import math

import numpy as np
import jax
import jax.numpy as jnp
from jax import lax
from jax.experimental import pallas as pl
from jax.experimental.pallas import tpu as pltpu

F32 = jnp.float32
BF16 = jnp.bfloat16

D_MODEL = 1024
N_MOD = 6
NORM_EPS = 1e-6
GRID_W = 64
S5_GROUP = 16
S5_GROUPS = D_MODEL // S5_GROUP
S5_STATE = 64
S5_CHUNK = 16
DIFF_HEAD_DIM = 64
DIFF_HEADS = D_MODEL // (2 * DIFF_HEAD_DIM)
DIFF_V_DIM = 2 * DIFF_HEAD_DIM
ROPE_BASE = 10000.0
FOURIER_GROUPS = 4
FOURIER_CH = D_MODEL // FOURIER_GROUPS

LANES = 128
VMEM_LIMIT_BYTES = 48 * 1024 * 1024


def _pick(n, pref):
    t = min(n, pref)
    while n % t:
        t //= 2
    return t


def _params(*sem):
    return pltpu.CompilerParams(dimension_semantics=sem, vmem_limit_bytes=VMEM_LIMIT_BYTES)


def _norm_mod(x, g, shift, scale):
    ms = jnp.mean(x * x, axis=-1, keepdims=True)
    return (x * lax.rsqrt(ms + NORM_EPS) * g) * (1.0 + scale) + shift


def _mod_spec(mod, rows_per_batch, tm):
    if mod.shape[0] == 1:
        return pl.BlockSpec((1, N_MOD, D_MODEL), lambda i, *_: (0, 0, 0))
    return pl.BlockSpec((1, N_MOD, D_MODEL), lambda i, *_: ((i * tm) // rows_per_batch, 0, 0))


def _mod_kernel(c_ref, w_ref, b_ref, o_ref):
    c = c_ref[...]
    s = c * jax.nn.sigmoid(c)
    o_ref[0] = jnp.dot(s.astype(BF16), w_ref[0].astype(BF16), preferred_element_type=F32) + b_ref[0]


def _modulation(cond, w_mod, b_mod):
    depth, _, n_out = w_mod.shape
    r = cond.shape[0]
    tn = _pick(n_out, 1536)
    return pl.pallas_call(
        _mod_kernel,
        grid=(depth, n_out // tn),
        in_specs=[pl.BlockSpec((r, D_MODEL), lambda i, j: (0, 0)),
                  pl.BlockSpec((1, D_MODEL, tn), lambda i, j: (i, 0, j)),
                  pl.BlockSpec((1, 1, tn), lambda i, j: (i, 0, j))],
        out_specs=pl.BlockSpec((1, r, tn), lambda i, j: (i, 0, j)),
        out_shape=jax.ShapeDtypeStruct((depth, r, n_out), F32),
        compiler_params=_params("parallel", "parallel"),
        name="adaln_modulation",
    )(cond, w_mod, b_mod.reshape(depth, 1, n_out))


def _mlp_kernel(x_ref, g_ref, mod_ref, w1_ref, w2_ref, o_ref, h_scr, acc_scr):
    j = pl.program_id(1)

    @pl.when(j == 0)
    def _():
        h = _norm_mod(x_ref[...], g_ref[...], mod_ref[0, 3:4, :], mod_ref[0, 4:5, :])
        h_scr[...] = h.astype(BF16)
        acc_scr[...] = jnp.zeros_like(acc_scr)

    a = jnp.dot(h_scr[...], w1_ref[...], preferred_element_type=F32)
    a = jnp.maximum(a, 0.0)
    a = a * a
    acc_scr[...] += jnp.dot(a.astype(BF16), w2_ref[...], preferred_element_type=F32)

    @pl.when(j == pl.num_programs(1) - 1)
    def _():
        o_ref[...] = x_ref[...] + mod_ref[0, 5:6, :] * acc_scr[...]


def _mlp(x, g, mod, w1, w2, rows_per_batch):
    m = x.shape[0]
    d_ff = w1.shape[1]
    tm = _pick(rows_per_batch, 1024)
    tf = _pick(d_ff, 512)
    return pl.pallas_call(
        _mlp_kernel,
        grid=(m // tm, d_ff // tf),
        in_specs=[pl.BlockSpec((tm, D_MODEL), lambda i, j: (i, 0)),
                  pl.BlockSpec((1, D_MODEL), lambda i, j: (0, 0)),
                  _mod_spec(mod, rows_per_batch, tm),
                  pl.BlockSpec((D_MODEL, tf), lambda i, j: (0, j)),
                  pl.BlockSpec((tf, D_MODEL), lambda i, j: (j, 0))],
        out_specs=pl.BlockSpec((tm, D_MODEL), lambda i, j: (i, 0)),
        out_shape=jax.ShapeDtypeStruct((m, D_MODEL), F32),
        scratch_shapes=[pltpu.VMEM((tm, D_MODEL), BF16), pltpu.VMEM((tm, D_MODEL), F32)],
        compiler_params=_params("parallel", "arbitrary"),
        name="sqrelu_mlp",
    )(x, g, mod, w1, w2)


def _linear_residual_kernel(a_ref, w_ref, b_ref, x_ref, mod_ref, o_ref, *, gate_row):
    y = jnp.dot(a_ref[...], w_ref[...], preferred_element_type=F32) + b_ref[...]
    o_ref[...] = x_ref[...] + mod_ref[0, gate_row:gate_row + 1, :] * y


def _linear_residual(a, w, b, x, mod, rows_per_batch, gate_row):
    m, k = a.shape
    tm = _pick(rows_per_batch, 512)
    return pl.pallas_call(
        functools.partial(_linear_residual_kernel, gate_row=gate_row),
        grid=(m // tm,),
        in_specs=[pl.BlockSpec((tm, k), lambda i: (i, 0)),
                  pl.BlockSpec((k, D_MODEL), lambda i: (0, 0)),
                  pl.BlockSpec((1, D_MODEL), lambda i: (0, 0)),
                  pl.BlockSpec((tm, D_MODEL), lambda i: (i, 0)),
                  _mod_spec(mod, rows_per_batch, tm)],
        out_specs=pl.BlockSpec((tm, D_MODEL), lambda i: (i, 0)),
        out_shape=jax.ShapeDtypeStruct((m, D_MODEL), F32),
        compiler_params=_params("parallel"),
        name="linear_residual",
    )(a, w, b, x, mod)


def _norm_mod_kernel(x_ref, g_ref, mod_ref, o_ref):
    h = _norm_mod(x_ref[...], g_ref[...], mod_ref[0, 0:1, :], mod_ref[0, 1:2, :])
    o_ref[...] = h.astype(o_ref.dtype)


def _norm_mod_call(x, g, mod, rows_per_batch):
    m = x.shape[0]
    tm = _pick(rows_per_batch, 1024)
    return pl.pallas_call(
        _norm_mod_kernel,
        grid=(m // tm,),
        in_specs=[pl.BlockSpec((tm, D_MODEL), lambda i: (i, 0)),
                  pl.BlockSpec((1, D_MODEL), lambda i: (0, 0)),
                  _mod_spec(mod, rows_per_batch, tm)],
        out_specs=pl.BlockSpec((tm, D_MODEL), lambda i: (i, 0)),
        out_shape=jax.ShapeDtypeStruct((m, D_MODEL), BF16),
        compiler_params=_params("parallel"),
        name="s5_norm_mod",
    )(x, g, mod)


def _s5_core_kernel(u_ref, m_ref, we_ref, ws_ref, a_ref, y_ref,
                    er, ei, sfr, sfi, sbr, sbi, *, n_ctx_chunks, n_chunks, batch):
    half = LANES // 2
    u = u_ref[0]
    e = jnp.dot(u, we_ref[0], preferred_element_type=F32)
    er[...] = e[:, :LANES].reshape(n_chunks, batch, LANES)
    ei[...] = e[:, LANES:].reshape(n_chunks, batch, LANES)
    ar = jnp.broadcast_to(a_ref[0, 0:1, :], (batch, LANES))
    ai = jnp.broadcast_to(a_ref[0, 1:2, :], (batch, LANES))
    is_fwd = lax.broadcasted_iota(jnp.int32, (batch, LANES), 1) < half

    def scan_segment(lo, hi, carry):
        def step(k, c):
            sr, si = c
            kb = lo + hi - 1 - k
            sfr[k] = sr
            sfi[k] = si
            sbr[kb] = sr
            sbi[kb] = si
            e_r = jnp.where(is_fwd, er[k], er[kb])
            e_i = jnp.where(is_fwd, ei[k], ei[kb])
            return (ar * sr - ai * si + e_r, ar * si + ai * sr + e_i)
        return lax.fori_loop(lo, hi, step, carry)

    zero = jnp.zeros((batch, LANES), F32)
    carry = scan_segment(0, n_ctx_chunks, (zero, zero))
    scan_segment(n_ctx_chunks, n_chunks, carry)

    is_fwd3 = lax.broadcasted_iota(jnp.int32, (n_chunks, batch, LANES), 2) < half
    s_re = jnp.where(is_fwd3, sfr[...], sbr[...]).reshape(n_chunks * batch, LANES)
    s_im = jnp.where(is_fwd3, sfi[...], sbi[...]).reshape(n_chunks * batch, LANES)
    s_cat = jnp.concatenate([s_re, s_im], axis=-1).astype(BF16)
    y_ref[0] = (jnp.dot(u, m_ref[0], preferred_element_type=F32)
                + jnp.dot(s_cat, ws_ref[0], preferred_element_type=F32))


def _s5_core(u, m_op, we_op, ws_op, a_op, n_ctx_chunks, n_chunks, batch):
    g, r, k = u.shape
    scr = pltpu.VMEM((n_chunks, batch, LANES), F32)
    return pl.pallas_call(
        functools.partial(_s5_core_kernel, n_ctx_chunks=n_ctx_chunks, n_chunks=n_chunks, batch=batch),
        grid=(g,),
        in_specs=[pl.BlockSpec((1, r, k), lambda i: (i, 0, 0)),
                  pl.BlockSpec((1, k, k), lambda i: (i, 0, 0)),
                  pl.BlockSpec((1, k, k), lambda i: (i, 0, 0)),
                  pl.BlockSpec((1, k, k), lambda i: (i, 0, 0)),
                  pl.BlockSpec((1, 2, LANES), lambda i: (i, 0, 0))],
        out_specs=pl.BlockSpec((1, r, k), lambda i: (i, 0, 0)),
        out_shape=jax.ShapeDtypeStruct((g, r, k), F32),
        scratch_shapes=[scr] * 6,
        compiler_params=_params("parallel"),
        name="s5_chunk_scan",
    )(u, m_op, we_op, ws_op, a_op)


def _s5_out_kernel(x_ref, y_ref, g_ref, mod_ref, d_ref, w_ref, o_ref):
    x = x_ref[...]
    h = _norm_mod(x, g_ref[...], mod_ref[0, 0:1, :], mod_ref[0, 1:2, :])
    z = jax.nn.gelu(y_ref[...] + d_ref[...] * h, approximate=True)
    gg = jnp.dot(z.astype(BF16), w_ref[...], preferred_element_type=F32)
    o = gg[:, :D_MODEL] * jax.nn.sigmoid(gg[:, D_MODEL:])
    o_ref[...] = x + mod_ref[0, 2:3, :] * o


def _s5_out(x, y, g, mod, d, w_glu, rows_per_batch):
    m = x.shape[0]
    tm = _pick(rows_per_batch, 512)
    return pl.pallas_call(
        _s5_out_kernel,
        grid=(m // tm,),
        in_specs=[pl.BlockSpec((tm, D_MODEL), lambda i: (i, 0)),
                  pl.BlockSpec((tm, D_MODEL), lambda i: (i, 0)),
                  pl.BlockSpec((1, D_MODEL), lambda i: (0, 0)),
                  _mod_spec(mod, rows_per_batch, tm),
                  pl.BlockSpec((1, D_MODEL), lambda i: (0, 0)),
                  pl.BlockSpec((D_MODEL, 2 * D_MODEL), lambda i: (0, 0))],
        out_specs=pl.BlockSpec((tm, D_MODEL), lambda i: (i, 0)),
        out_shape=jax.ShapeDtypeStruct((m, D_MODEL), F32),
        compiler_params=_params("parallel"),
        name="s5_glu_out",
    )(x, y, g, mod, d, w_glu)


def _s5_operators(lam_re, lam_im, log_dt, b_re, b_im, c_re, c_im):
    hp = lax.Precision.HIGHEST
    t_len = S5_CHUNK
    lam_re = lam_re.astype(F32)
    lam_im = lam_im.astype(F32)
    dt = jnp.exp(log_dt.astype(F32))[..., None]
    mag = jnp.exp(lam_re * dt)
    a_re = mag * jnp.cos(lam_im * dt)
    a_im = mag * jnp.sin(lam_im * dt)
    n_re = a_re - 1.0
    n_im = a_im
    den = lam_re * lam_re + lam_im * lam_im
    k_re = (n_re * lam_re + n_im * lam_im) / den
    k_im = (n_im * lam_re - n_re * lam_im) / den
    b_re = b_re.astype(F32)
    b_im = b_im.astype(F32)
    bb_re = k_re[..., None] * b_re - k_im[..., None] * b_im
    bb_im = k_re[..., None] * b_im + k_im[..., None] * b_re
    c_re = c_re.astype(F32)
    c_im = c_im.astype(F32)

    pr = [jnp.ones_like(a_re)]
    pi = [jnp.zeros_like(a_im)]
    for _ in range(t_len):
        pr.append(pr[-1] * a_re - pi[-1] * a_im)
        pi.append(pr[-2] * a_im + pi[-1] * a_re)
    pr = jnp.stack(pr)
    pi = jnp.stack(pi)

    car = c_re[None] * pr[:, :, :, None, :] - c_im[None] * pi[:, :, :, None, :]
    cai = c_re[None] * pi[:, :, :, None, :] + c_im[None] * pr[:, :, :, None, :]
    kern = (jnp.einsum('ldgop,dgph->ldgoh', car, bb_re, precision=hp)
            - jnp.einsum('ldgop,dgph->ldgoh', cai, bb_im, precision=hp))

    s_idx = np.arange(t_len)[:, None]
    t_idx = np.arange(t_len)[None, :]
    lag_f = np.clip(t_idx - s_idx, 0, t_len)
    lag_b = np.clip(s_idx - t_idx, 0, t_len)
    mask_f = jnp.asarray((s_idx <= t_idx).astype(np.float32))[:, :, None, None, None]
    mask_b = jnp.asarray((s_idx >= t_idx).astype(np.float32))[:, :, None, None, None]
    mm = kern[lag_f, 0] * mask_f + kern[lag_b, 1] * mask_b
    g = mm.shape[2]
    th = t_len * S5_GROUP
    m_op = mm.transpose(2, 0, 4, 1, 3).reshape(g, th, th)

    abr = pr[:t_len, :, :, :, None] * bb_re[None] - pi[:t_len, :, :, :, None] * bb_im[None]
    abi = pr[:t_len, :, :, :, None] * bb_im[None] + pi[:t_len, :, :, :, None] * bb_re[None]

    def to_rows(z):
        return z.transpose(1, 0, 3, 2).reshape(g, th, S5_STATE)
    we_op = jnp.concatenate([to_rows(abr[::-1, 0]), to_rows(abr[:, 1]),
                             to_rows(abi[::-1, 0]), to_rows(abi[:, 1])], axis=-1)

    def to_cols(z):
        return z.transpose(1, 3, 0, 2).reshape(g, S5_STATE, th)
    ws_op = jnp.concatenate([to_cols(car[1:, 0]), to_cols(car[1:, 1][::-1]),
                             to_cols(-cai[1:, 0]), to_cols(-cai[1:, 1][::-1])], axis=1)

    a_op = jnp.stack([jnp.concatenate([pr[t_len, 0], pr[t_len, 1]], axis=-1),
                      jnp.concatenate([pi[t_len, 0], pi[t_len, 1]], axis=-1)], axis=1)
    return m_op.astype(BF16), we_op.astype(BF16), ws_op.astype(BF16), a_op


def _s5_layer(x_l, x_c, g, mod_l, mod_c, ops, d, w_glu, bsz, n_lat, n_ctx, ctx_out):
    h_l = _norm_mod_call(x_l, g, mod_l, n_lat).reshape(bsz, n_lat, D_MODEL)
    h_c = _norm_mod_call(x_c, g, mod_c, n_ctx).reshape(bsz, n_ctx, D_MODEL)
    n_tok = n_ctx + n_lat
    n_chunks = n_tok // S5_CHUNK
    h_all = jnp.concatenate([h_c, h_l], axis=1)
    u = h_all.reshape(bsz, n_chunks, S5_CHUNK, S5_GROUPS, S5_GROUP)
    u = u.transpose(3, 1, 0, 2, 4).reshape(S5_GROUPS, n_chunks * bsz, S5_CHUNK * S5_GROUP)
    y = _s5_core(u, *ops, n_ctx // S5_CHUNK, n_chunks, bsz)
    y = y.reshape(S5_GROUPS, n_chunks, bsz, S5_CHUNK, S5_GROUP)
    y = y.transpose(2, 1, 3, 0, 4).reshape(bsz, n_tok, D_MODEL)
    y_l = y[:, n_ctx:].reshape(bsz * n_lat, D_MODEL)
    x_l = _s5_out(x_l, y_l, g, mod_l, d, w_glu, n_lat)
    if ctx_out:
        y_c = y[:, :n_ctx].reshape(bsz * n_ctx, D_MODEL)
        x_c = _s5_out(x_c, y_c, g, mod_c, d, w_glu, n_ctx)
    return x_l, x_c


def _lane_order():
    which, half, axis, f = np.meshgrid(np.arange(2), np.arange(2), np.arange(2), np.arange(16), indexing="ij")
    head_dim = (axis * 32 + which * 16 + f).reshape(-1)
    return (half.reshape(-1) * DIFF_HEAD_DIM + head_dim), head_dim


def _first_half_mask(shape):
    lane = lax.broadcasted_iota(jnp.int32, shape, len(shape) - 1)
    return (lane % 64) < 32


def _qkv_kernel(x_ref, g_ref, mod_ref, w_ref, gs_ref, ones_ref, cos_ref, sin_ref, o_ref, h_scr, *, rope):
    j = pl.program_id(1)

    @pl.when(j == 0)
    def _():
        h = _norm_mod(x_ref[...], g_ref[...], mod_ref[0, 0:1, :], mod_ref[0, 1:2, :])
        h_scr[...] = h.astype(BF16)

    y = jnp.dot(h_scr[...], w_ref[...], preferred_element_type=F32)

    @pl.when(j == 2)
    def _():
        o_ref[0] = y.astype(BF16)

    @pl.when(j < 2)
    def _():
        gs = gs_ref[0]
        ones = ones_ref[...]
        for hh in range(DIFF_HEADS):
            t = y[:, hh * LANES:(hh + 1) * LANES]
            t2 = t * t
            hi = t2.astype(BF16)
            lo = (t2 - hi.astype(F32)).astype(BF16)
            ss = (jnp.dot(hi, ones, preferred_element_type=F32)
                  + jnp.dot(lo, ones, preferred_element_type=F32))
            t = t * lax.rsqrt(ss * (1.0 / DIFF_HEAD_DIM) + NORM_EPS) * gs
            if rope:
                t = t * cos_ref[...] + pltpu.roll(t, LANES // 2, 1) * sin_ref[...]
            o_ref[0, :, hh * LANES:(hh + 1) * LANES] = t.astype(BF16)


def _qkv(x, g, mod, w_qkv, gs, ones, cos_t, sin_t, rows_per_batch, rope):
    m = x.shape[0]
    tm = _pick(rows_per_batch, 512)
    nblk = rows_per_batch // tm
    return pl.pallas_call(
        functools.partial(_qkv_kernel, rope=rope),
        grid=(m // tm, 3),
        in_specs=[pl.BlockSpec((tm, D_MODEL), lambda i, j: (i, 0)),
                  pl.BlockSpec((1, D_MODEL), lambda i, j: (0, 0)),
                  _mod_spec(mod, rows_per_batch, tm),
                  pl.BlockSpec((D_MODEL, D_MODEL), lambda i, j: (0, j)),
                  pl.BlockSpec((1, 1, LANES), lambda i, j: (jnp.minimum(j, 1), 0, 0)),
                  pl.BlockSpec((LANES, LANES), lambda i, j: (0, 0)),
                  pl.BlockSpec((tm, LANES), lambda i, j: (i % nblk, 0)),
                  pl.BlockSpec((tm, LANES), lambda i, j: (i % nblk, 0))],
        out_specs=pl.BlockSpec((1, tm, D_MODEL), lambda i, j: (j, i, 0)),
        out_shape=jax.ShapeDtypeStruct((3, m, D_MODEL), BF16),
        scratch_shapes=[pltpu.VMEM((tm, D_MODEL), BF16)],
        compiler_params=_params("parallel", "arbitrary"),
        name="diff_qkv",
    )(x, g, mod, w_qkv, gs, ones, cos_t, sin_t)


def _flash_kernel(stab_ref, q_ref, k_ref, v_ref, lam_ref, sub_ref, o_ref, vaug_scr, *, tk, lam_init, online):
    tq = q_ref.shape[1]
    nk = k_ref.shape[1]
    n_chunks = nk // tk

    @pl.when(pl.program_id(2) == 0)
    def _():
        vaug_scr[:, :DIFF_V_DIM] = v_ref[0]
        vaug_scr[:, DIFF_V_DIM:] = jnp.ones((nk, DIFF_V_DIM), BF16)

    q = q_ref[0]
    first = _first_half_mask(q.shape)
    zero = jnp.zeros_like(q)
    qq = jnp.concatenate([jnp.where(first, q, zero), jnp.where(first, zero, q)], axis=0)

    def scores(j):
        off = j * tk if isinstance(j, int) else pl.multiple_of(j * tk, tk)
        kk = k_ref[0, pl.ds(off, tk), :]
        return lax.dot_general(qq, kk, (((1,), (1,)), ((), ())), preferred_element_type=F32)

    if online:
        def update(s, j, state):
            m, acc = state
            off = j * tk if isinstance(j, int) else pl.multiple_of(j * tk, tk)
            m_new = jnp.maximum(m, jnp.max(s, axis=-1, keepdims=True))
            alpha = jnp.exp2(m - m_new)
            p = jnp.exp2(s - m_new).astype(BF16)
            acc = alpha * acc + jnp.dot(p, vaug_scr[pl.ds(off, tk), :], preferred_element_type=F32)
            return m_new, acc
        state = (jnp.full((2 * tq, 1), -jnp.inf, F32), jnp.zeros((2 * tq, 2 * DIFF_V_DIM), F32))
    else:
        stab = stab_ref[...]

        def update(s, j, acc):
            off = j * tk if isinstance(j, int) else pl.multiple_of(j * tk, tk)
            p = jnp.exp2(s - stab).astype(BF16)
            return acc + jnp.dot(p, vaug_scr[pl.ds(off, tk), :], preferred_element_type=F32)
        state = jnp.zeros((2 * tq, 2 * DIFF_V_DIM), F32)

    if online:
        def pair(jj, carry):
            s_a, st = carry
            j0 = 2 * jj
            s_b = scores(j0 + 1)
            st = update(s_a, j0, st)
            s_a = scores(j0 + 2)
            st = update(s_b, j0 + 1, st)
            return s_a, st

        n_pairs = (n_chunks - 1) // 2
        s_a, state = lax.fori_loop(0, n_pairs, pair, (scores(0), state))
        done = 2 * n_pairs
        if n_chunks - done == 2:
            s_b = scores(done + 1)
            state = update(s_a, done, state)
            state = update(s_b, done + 1, state)
        else:
            state = update(s_a, done, state)
        acc = state[1]
    else:
        s_cur = scores(0)
        for j in range(n_chunks):
            s_next = scores(j + 1) if j + 1 < n_chunks else None
            state = update(s_cur, j, state)
            s_cur = s_next
        acc = state

    lp = lam_ref[...]
    lam = (jnp.exp(jnp.sum(lp[0:1] * lp[1:2], axis=-1, keepdims=True))
           - jnp.exp(jnp.sum(lp[2:3] * lp[3:4], axis=-1, keepdims=True)) + lam_init)
    o = (acc[:tq, :DIFF_V_DIM] / acc[:tq, DIFF_V_DIM:]
         - lam * (acc[tq:, :DIFF_V_DIM] / acc[tq:, DIFF_V_DIM:]))
    ms = jnp.mean(o * o, axis=-1, keepdims=True)
    o = o * lax.rsqrt(ms + NORM_EPS) * sub_ref[...] * (1.0 - lam_init)
    o_ref[0] = o.astype(o_ref.dtype)


def _flash(stab, q, k, v, lam_params, subln, lam_init, online):
    bsz, nq, _ = q.shape
    nk = k.shape[1]
    tq = _pick(nq, 512)
    tk = _pick(nk, 256)
    return pl.pallas_call(
        functools.partial(_flash_kernel, tk=tk, lam_init=lam_init, online=online),
        grid=(bsz, DIFF_HEADS, nq // tq),
        in_specs=[pl.BlockSpec((1, 1), lambda b, h, i: (0, 0)),
                  pl.BlockSpec((1, tq, LANES), lambda b, h, i: (b, i, h)),
                  pl.BlockSpec((1, nk, LANES), lambda b, h, i: (b, 0, h)),
                  pl.BlockSpec((1, nk, LANES), lambda b, h, i: (b, 0, h)),
                  pl.BlockSpec((4, DIFF_HEAD_DIM), lambda b, h, i: (0, 0)),
                  pl.BlockSpec((1, DIFF_V_DIM), lambda b, h, i: (0, 0))],
        out_specs=pl.BlockSpec((1, tq, LANES), lambda b, h, i: (b, i, h)),
        out_shape=jax.ShapeDtypeStruct((bsz, nq, D_MODEL), BF16),
        scratch_shapes=[pltpu.VMEM((nk, 2 * DIFF_V_DIM), BF16)],
        compiler_params=_params("parallel", "parallel", "arbitrary"),
        name="diff_flash_attention",
    )(stab, q, k, v, lam_params, subln)


_MAX_FIXED_STABILISER = 40.0


def _attend(stab, q, k, v, lam_params, subln, lam_init):
    return lax.cond(stab[0, 0] <= _MAX_FIXED_STABILISER,
                    lambda: _flash(stab, q, k, v, lam_params, subln, lam_init, False),
                    lambda: _flash(stab, q, k, v, lam_params, subln, lam_init, True))


def _rope_tables(n_tokens):
    rows = n_tokens // GRID_W
    row = jnp.repeat(jnp.arange(rows, dtype=F32), GRID_W)
    col = jnp.tile(jnp.arange(GRID_W, dtype=F32), rows)
    half = DIFF_HEAD_DIM // 2
    inv = jnp.power(ROPE_BASE, -jnp.arange(0, half, 2, dtype=F32) / half)
    ang = jnp.concatenate([row[:, None] * inv, col[:, None] * inv], axis=-1)
    ang = jnp.tile(ang, (1, 4))
    sign = jnp.asarray(np.repeat([-1.0, 1.0], LANES // 2), F32)
    return jnp.cos(ang), jnp.sin(ang) * sign


def _diff_layer(x_l, x_c, g, mod_l, mod_c, w_qkv, q_norm, k_norm, lam_params, subln, w_o,
                lam_init, bsz, n_lat, n_ctx, ctx_out):
    cos_t, sin_t = _rope_tables(n_lat)
    tile_src, head_dim = _lane_order()
    q_scale = DIFF_HEAD_DIM ** -0.5 * math.log2(math.e)
    qn = q_norm.astype(F32)
    kn = k_norm.astype(F32)
    gs = jnp.stack([qn[head_dim] * q_scale, kn[head_dim]])[:, None, :]
    half_id = (np.arange(LANES) % 64) // 32
    ones = jnp.asarray(half_id[:, None] == half_id[None, :], F32).astype(BF16)
    stab = (1.02 * DIFF_HEAD_DIM * q_scale * jnp.max(jnp.abs(qn)) * jnp.max(jnp.abs(kn))).reshape(1, 1)

    def permuted(w):
        return w.reshape(D_MODEL, DIFF_HEADS, LANES)[:, :, tile_src].reshape(D_MODEL, D_MODEL)
    w = jnp.concatenate([permuted(w_qkv[:, :D_MODEL]), permuted(w_qkv[:, D_MODEL:2 * D_MODEL]),
                         w_qkv[:, 2 * D_MODEL:]], axis=1).astype(BF16)
    qkv_l = _qkv(x_l, g, mod_l, w, gs, ones, cos_t, sin_t, n_lat, True).reshape(3, bsz, n_lat, D_MODEL)
    qkv_c = _qkv(x_c, g, mod_c, w, gs, ones, cos_t, sin_t, n_ctx, False).reshape(3, bsz, n_ctx, D_MODEL)
    k_all = jnp.concatenate([qkv_l[1], qkv_c[1]], axis=1)
    v_all = jnp.concatenate([qkv_l[2], qkv_c[2]], axis=1)
    lam_p = lam_params.astype(F32)
    sub = subln.astype(F32)[None]
    wo = w_o.astype(BF16)
    zero_b = jnp.zeros((1, D_MODEL), F32)
    o_l = _attend(stab, qkv_l[0], k_all, v_all, lam_p, sub, lam_init).reshape(bsz * n_lat, D_MODEL)
    x_l = _linear_residual(o_l, wo, zero_b, x_l, mod_l, n_lat, 2)
    if ctx_out:
        o_c = _attend(stab, qkv_c[0], qkv_c[1], qkv_c[2], lam_p, sub, lam_init).reshape(bsz * n_ctx, D_MODEL)
        x_c = _linear_residual(o_c, wo, zero_b, x_c, mod_c, n_ctx, 2)
    return x_l, x_c


def _fft1_kernel(x_ref, g_ref, mod_ref, t_ref, o_ref, *, n1, group):
    gamma = g_ref[...]
    shift = mod_ref[0, 0:1, :]
    scale = mod_ref[0, 1:2, :]
    for jj in range(group):
        sl = slice(jj * D_MODEL, (jj + 1) * D_MODEL)
        h = _norm_mod(x_ref[0, :, sl], gamma, shift, scale).astype(BF16)
        a = jnp.dot(t_ref[jj], h, preferred_element_type=F32)
        o_ref[0, 0, :, sl] = a[:n1].astype(BF16)
        o_ref[0, 1, :, sl] = a[n1:].astype(BF16)


def _fft2_kernel(a_ref, x_ref, mod_ref, f2_ref, cd_ref, w_ref, b_ref, o_ref, zr_scr, zi_scr, *, n2, group):
    for kk in range(group):
        a = jnp.concatenate([a_ref[0, 0, kk], a_ref[0, 1, kk]], axis=0)
        z = jnp.dot(f2_ref[...], a, preferred_element_type=F32)
        zr_scr[kk * n2:(kk + 1) * n2, :] = z[:n2].astype(BF16)
        zi_scr[kk * n2:(kk + 1) * n2, :] = z[n2:].astype(BF16)
    zr = zr_scr[...]
    zi = zi_scr[...]
    fs = []
    for cg in range(FOURIER_GROUPS):
        sl = slice(cg * FOURIER_CH, (cg + 1) * FOURIER_CH)
        fs.append(jnp.dot(zr[:, sl], cd_ref[0], preferred_element_type=F32)
                  + jnp.dot(zi[:, sl], cd_ref[1], preferred_element_type=F32))
    f = jnp.concatenate(fs, axis=-1).astype(BF16)
    o = jnp.dot(f, w_ref[...], preferred_element_type=F32) + b_ref[...]
    gate = mod_ref[0, 2:3, :]
    for kk in range(group):
        sl = slice(kk * D_MODEL, (kk + 1) * D_MODEL)
        o_ref[0, :, sl] = x_ref[0, :, sl] + gate * o[kk * n2:(kk + 1) * n2]


def _fourier_tables(n):
    n1 = int(round(math.sqrt(n)))
    n2 = n // n1
    assert n1 * n2 == n and n1 % 8 == 0 and n2 % 8 == 0
    k1 = np.arange(n1)[None, :, None]
    t1 = np.arange(n1)[None, None, :]
    t2 = np.arange(n2)[:, None, None]
    ang = 2.0 * np.pi * ((k1 * (n2 * t1 + t2)) % n) / n
    t_tab = np.concatenate([np.cos(ang), -np.sin(ang)], axis=1)
    k2 = np.arange(n2)[:, None]
    tt = np.arange(n2)[None, :]
    ang2 = 2.0 * np.pi * ((k2 * tt) % n2) / n2
    c2, s2 = np.cos(ang2), np.sin(ang2)
    f2 = np.block([[c2, s2], [-s2, c2]])
    cc = np.arange(FOURIER_CH)
    ang3 = 2.0 * np.pi * ((cc[:, None] * cc[None, :]) % FOURIER_CH) / FOURIER_CH
    norm = 1.0 / math.sqrt(n * FOURIER_CH)
    cd = np.stack([np.cos(ang3), np.sin(ang3)]) * norm
    as_bf16 = lambda z: jnp.asarray(z, dtype=F32).astype(BF16)
    return n1, n2, as_bf16(t_tab), as_bf16(f2), as_bf16(cd)


def _fourier_layer_one(x, g, mod, w_f, b_f, bsz, n):
    n1, n2, t_tab, f2, cd = _fourier_tables(n)
    g1 = _pick(n2, 8)
    g2 = _pick(n1, 8)
    batch_idx = (lambda b: b) if mod.shape[0] > 1 else (lambda b: 0)
    mod_spec = pl.BlockSpec((1, N_MOD, D_MODEL), lambda b, j: (batch_idx(b), 0, 0))
    a = pl.pallas_call(
        functools.partial(_fft1_kernel, n1=n1, group=g1),
        grid=(bsz, n2 // g1),
        in_specs=[pl.BlockSpec((1, n1, g1 * D_MODEL), lambda b, j: (b, 0, j)),
                  pl.BlockSpec((1, D_MODEL), lambda b, j: (0, 0)),
                  mod_spec,
                  pl.BlockSpec((g1, 2 * n1, n1), lambda b, j: (j, 0, 0))],
        out_specs=pl.BlockSpec((1, 2, n1, g1 * D_MODEL), lambda b, j: (b, 0, 0, j)),
        out_shape=jax.ShapeDtypeStruct((bsz, 2, n1, n2 * D_MODEL), BF16),
        compiler_params=_params("parallel", "parallel"),
        name="fourier_token_dft_stage1",
    )(x.reshape(bsz, n1, n2 * D_MODEL), g, mod, t_tab)
    a = a.reshape(bsz, 2, n1, n2, D_MODEL)
    out = pl.pallas_call(
        functools.partial(_fft2_kernel, n2=n2, group=g2),
        grid=(bsz, n1 // g2),
        in_specs=[pl.BlockSpec((1, 2, g2, n2, D_MODEL), lambda b, j: (b, 0, j, 0, 0)),
                  pl.BlockSpec((1, n2, g2 * D_MODEL), lambda b, j: (b, 0, j)),
                  mod_spec,
                  pl.BlockSpec((2 * n2, 2 * n2), lambda b, j: (0, 0)),
                  pl.BlockSpec((2, FOURIER_CH, FOURIER_CH), lambda b, j: (0, 0, 0)),
                  pl.BlockSpec((D_MODEL, D_MODEL), lambda b, j: (0, 0)),
                  pl.BlockSpec((1, D_MODEL), lambda b, j: (0, 0))],
        out_specs=pl.BlockSpec((1, n2, g2 * D_MODEL), lambda b, j: (b, 0, j)),
        out_shape=jax.ShapeDtypeStruct((bsz, n2, n1 * D_MODEL), F32),
        scratch_shapes=[pltpu.VMEM((g2 * n2, D_MODEL), BF16), pltpu.VMEM((g2 * n2, D_MODEL), BF16)],
        compiler_params=_params("parallel", "parallel"),
        name="fourier_token_dft_stage2",
    )(a, x.reshape(bsz, n2, n1 * D_MODEL), mod, f2, cd, w_f, b_f)
    return out.reshape(bsz * n, D_MODEL)


def kernel(x, c, ctx, c_ctx, w_mod, b_mod, norm_g, mlp_w1, mlp_w2, s5_lambda_re, s5_lambda_im, s5_log_dt, s5_b_re, s5_b_im, s5_c_re, s5_c_im, s5_d, s5_w_glu, diff_w_qkv, diff_q_norm, diff_k_norm, diff_lambda, diff_subln, diff_w_o, fourier_w, fourier_b):
    bsz, n_lat, _ = x.shape
    n_ctx = ctx.shape[1]
    depth = w_mod.shape[0]
    assert bsz == 8 and n_lat % GRID_W == 0 and n_ctx % S5_CHUNK == 0 and n_lat % S5_CHUNK == 0

    cond_rows = 16
    cond = jnp.concatenate([c.astype(F32), c_ctx.astype(F32)[None],
                            jnp.zeros((cond_rows - bsz - 1, D_MODEL), F32)], axis=0)
    mod_all = _modulation(cond, w_mod, b_mod).reshape(depth, cond_rows, N_MOD, D_MODEL)

    x_l = x.reshape(bsz * n_lat, D_MODEL)
    x_c = ctx.reshape(bsz * n_ctx, D_MODEL)
    for i in range(depth):
        last = i == depth - 1
        mod_l = mod_all[i, :bsz]
        mod_c = mod_all[i, bsz:bsz + 1]
        g_mix = norm_g[i, 0][None]
        g_mlp = norm_g[i, 1][None]
        kind = i % 3
        j = i // 3
        if kind == 0:
            ops = _s5_operators(s5_lambda_re[j], s5_lambda_im[j], s5_log_dt[j], s5_b_re[j], s5_b_im[j],
                                s5_c_re[j], s5_c_im[j])
            x_l, x_c = _s5_layer(x_l, x_c, g_mix, mod_l, mod_c, ops, s5_d[j][None],
                                 s5_w_glu[j].astype(BF16), bsz, n_lat, n_ctx, not last)
        elif kind == 1:
            lam_init = 0.8 - 0.6 * math.exp(-0.3 * i)
            x_l, x_c = _diff_layer(x_l, x_c, g_mix, mod_l, mod_c, diff_w_qkv[j], diff_q_norm[j],
                                   diff_k_norm[j], diff_lambda[j], diff_subln[j], diff_w_o[j],
                                   lam_init, bsz, n_lat, n_ctx, not last)
        else:
            w_f = fourier_w[j].astype(BF16)
            b_f = fourier_b[j][None]
            x_l = _fourier_layer_one(x_l, g_mix, mod_l, w_f, b_f, bsz, n_lat)
            if not last:
                x_c = _fourier_layer_one(x_c, g_mix, mod_c, w_f, b_f, bsz, n_ctx)
        w1 = mlp_w1[i].astype(BF16)
        w2 = mlp_w2[i].astype(BF16)
        x_l = _mlp(x_l, g_mlp, mod_l, w1, w2, n_lat)
        if not last:
            x_c = _mlp(x_c, g_mlp, mod_c, w1, w2, n_ctx)
    return x_l.reshape(bsz, n_lat, D_MODEL)
```

```python
import functools
import math

import numpy as np
import jax
import jax.numpy as jnp
from jax import lax
from jax.experimental import pallas as pl
from jax.experimental.pallas import tpu as pltpu

F32 = jnp.float32
BF16 = jnp.bfloat16

D_MODEL = 1024
N_MOD = 6
NORM_EPS = 1e-6
GRID_W = 64
S5_GROUP = 16
S5_GROUPS = D_MODEL // S5_GROUP
S5_STATE = 64
DIFF_HEAD_DIM = 64
DIFF_HEADS = D_MODEL // (2 * DIFF_HEAD_DIM)
DIFF_V_DIM = 2 * DIFF_HEAD_DIM
ROPE_BASE = 10000.0
FOURIER_GROUPS = 4
FOURIER_CH = D_MODEL // FOURIER_GROUPS

LANES = 128
VMEM_LIMIT_BYTES = 48 * 1024 * 1024


def _pick(n, pref):
    t = min(n, pref)
    while n % t:
        t //= 2
    return t


def _params(*sem):
    return pltpu.CompilerParams(dimension_semantics=sem, vmem_limit_bytes=VMEM_LIMIT_BYTES)


def _norm_mod(x, g, shift, scale):
    ms = jnp.mean(x * x, axis=-1, keepdims=True)
    return (x * lax.rsqrt(ms + NORM_EPS) * g) * (1.0 + scale) + shift


def _mod_spec(mod, rows_per_batch, tm):
    if mod.shape[0] == 1:
        return pl.BlockSpec((1, N_MOD, D_MODEL), lambda i, *_: (0, 0, 0))
    return pl.BlockSpec((1, N_MOD, D_MODEL), lambda i, *_: ((i * tm) // rows_per_batch, 0, 0))


def _mod_kernel(c_ref, w_ref, b_ref, o_ref):
    c = c_ref[...]
    s = c * jax.nn.sigmoid(c)
    o_ref[0] = jnp.dot(s.astype(BF16), w_ref[0].astype(BF16), preferred_element_type=F32) + b_ref[0]


def _modulation(cond, w_mod, b_mod):
    depth, _, n_out = w_mod.shape
    r = cond.shape[0]
    tn = _pick(n_out, 1536)
    return pl.pallas_call(
        _mod_kernel,
        grid=(depth, n_out // tn),
        in_specs=[pl.BlockSpec((r, D_MODEL), lambda i, j: (0, 0)),
                  pl.BlockSpec((1, D_MODEL, tn), lambda i, j: (i, 0, j)),
                  pl.BlockSpec((1, 1, tn), lambda i, j: (i, 0, j))],
        out_specs=pl.BlockSpec((1, r, tn), lambda i, j: (i, 0, j)),
        out_shape=jax.ShapeDtypeStruct((depth, r, n_out), F32),
        compiler_params=_params("parallel", "parallel"),
        name="adaln_modulation",
    )(cond, w_mod, b_mod.reshape(depth, 1, n_out))


def _mlp_kernel(x_ref, g_ref, mod_ref, w1_ref, w2_ref, o_ref, h_scr, acc_scr):
    j = pl.program_id(1)

    @pl.when(j == 0)
    def _():
        h = _norm_mod(x_ref[...], g_ref[...], mod_ref[0, 3:4, :], mod_ref[0, 4:5, :])
        h_scr[...] = h.astype(BF16)
        acc_scr[...] = jnp.zeros_like(acc_scr)

    a = jnp.dot(h_scr[...], w1_ref[...], preferred_element_type=F32)
    a = jnp.maximum(a, 0.0)
    a = a * a
    acc_scr[...] += jnp.dot(a.astype(BF16), w2_ref[...], preferred_element_type=F32)

    @pl.when(j == pl.num_programs(1) - 1)
    def _():
        o_ref[...] = x_ref[...] + mod_ref[0, 5:6, :] * acc_scr[...]


def _mlp(x, g, mod, w1, w2, rows_per_batch):
    m = x.shape[0]
    d_ff = w1.shape[1]
    tm = _pick(rows_per_batch, 1024)
    tf = _pick(d_ff, 512)
    return pl.pallas_call(
        _mlp_kernel,
        grid=(m // tm, d_ff // tf),
        in_specs=[pl.BlockSpec((tm, D_MODEL), lambda i, j: (i, 0)),
                  pl.BlockSpec((1, D_MODEL), lambda i, j: (0, 0)),
                  _mod_spec(mod, rows_per_batch, tm),
                  pl.BlockSpec((D_MODEL, tf), lambda i, j: (0, j)),
                  pl.BlockSpec((tf, D_MODEL), lambda i, j: (j, 0))],
        out_specs=pl.BlockSpec((tm, D_MODEL), lambda i, j: (i, 0)),
        out_shape=jax.ShapeDtypeStruct((m, D_MODEL), F32),
        scratch_shapes=[pltpu.VMEM((tm, D_MODEL), BF16), pltpu.VMEM((tm, D_MODEL), F32)],
        compiler_params=_params("parallel", "arbitrary"),
        name="sqrelu_mlp",
    )(x, g, mod, w1, w2)


def _linear_residual_kernel(a_ref, w_ref, b_ref, x_ref, mod_ref, o_ref, *, gate_row):
    y = jnp.dot(a_ref[...], w_ref[...], preferred_element_type=F32) + b_ref[...]
    o_ref[...] = x_ref[...] + mod_ref[0, gate_row:gate_row + 1, :] * y


def _linear_residual(a, w, b, x, mod, rows_per_batch, gate_row):
    m, k = a.shape
    tm = _pick(rows_per_batch, 512)
    return pl.pallas_call(
        functools.partial(_linear_residual_kernel, gate_row=gate_row),
        grid=(m // tm,),
        in_specs=[pl.BlockSpec((tm, k), lambda i: (i, 0)),
                  pl.BlockSpec((k, D_MODEL), lambda i: (0, 0)),
                  pl.BlockSpec((1, D_MODEL), lambda i: (0, 0)),
                  pl.BlockSpec((tm, D_MODEL), lambda i: (i, 0)),
                  _mod_spec(mod, rows_per_batch, tm)],
        out_specs=pl.BlockSpec((tm, D_MODEL), lambda i: (i, 0)),
        out_shape=jax.ShapeDtypeStruct((m, D_MODEL), F32),
        compiler_params=_params("parallel"),
        name="linear_residual",
    )(a, w, b, x, mod)


S5_SLAB_CH = 256
S5_SLABS = D_MODEL // S5_SLAB_CH
S5_SLAB_GROUPS = S5_SLAB_CH // S5_GROUP
S5_SLAB_STATE = S5_SLAB_GROUPS * S5_STATE
S5_TIME_BLOCK = 32


def _norm_mod_kernel(x_ref, g_ref, mod_ref, o_ref):
    h = _norm_mod(x_ref[...], g_ref[...], mod_ref[0, 0:1, :], mod_ref[0, 1:2, :])
    o_ref[...] = h.astype(o_ref.dtype)


def _batch_mod_spec(mod):
    if mod.shape[0] == 1:
        return pl.BlockSpec((1, N_MOD, D_MODEL), lambda b, i: (0, 0, 0))
    return pl.BlockSpec((1, N_MOD, D_MODEL), lambda b, i: (b, 0, 0))


def _norm_mod_time_major(x, g, mod, bsz, n):
    tm = _pick(n, 1024)
    nblk = n // tm
    return pl.pallas_call(
        _norm_mod_kernel,
        grid=(bsz, nblk),
        in_specs=[pl.BlockSpec((tm, D_MODEL), lambda b, i: (b * nblk + i, 0)),
                  pl.BlockSpec((1, D_MODEL), lambda b, i: (0, 0)),
                  _batch_mod_spec(mod)],
        out_specs=pl.BlockSpec((tm, D_MODEL), lambda b, i: (i, b)),
        out_shape=jax.ShapeDtypeStruct((n, bsz * D_MODEL), BF16),
        compiler_params=_params("parallel", "parallel"),
        name="s5_norm_mod",
    )(x, g, mod)


def _s5_scan_kernel(*refs, tt, batch, reverse, accumulate):
    if accumulate:
        u_ref, yin_ref, w1_ref, w3_ref, a_ref, y_ref, bu_scr, state_scr = refs
    else:
        u_ref, w1_ref, w3_ref, a_ref, y_ref, bu_scr, state_scr = refs
    ns = S5_SLAB_STATE

    @pl.when(pl.program_id(0) == 0)
    def _():
        state_scr[...] = jnp.zeros_like(state_scr)

    for s in range(S5_SLABS):
        sl = slice(s * S5_SLAB_CH, (s + 1) * S5_SLAB_CH)
        bu_scr[s] = jnp.dot(u_ref[:, sl], w1_ref[s], preferred_element_type=F32)
        ar = jnp.broadcast_to(a_ref[s, 0:1, :], (batch, ns))
        ai = jnp.broadcast_to(a_ref[s, 1:2, :], (batch, ns))
        hr = state_scr[s, :, :ns]
        hi = state_scr[s, :, ns:]
        for t in (range(tt - 1, -1, -1) if reverse else range(tt)):
            rows = slice(t * batch, (t + 1) * batch)
            hr, hi = (ar * hr - ai * hi + bu_scr[s, rows, :ns],
                      ar * hi + ai * hr + bu_scr[s, rows, ns:])
            bu_scr[s, rows, :ns] = hr
            bu_scr[s, rows, ns:] = hi
        state_scr[s, :, :ns] = hr
        state_scr[s, :, ns:] = hi
        y = jnp.dot(bu_scr[s].astype(BF16), w3_ref[s], preferred_element_type=F32)
        if accumulate:
            y = y + yin_ref[:, sl]
        y_ref[:, sl] = y


def _s5_scan(u, y_in, w1, w3, a, n_first_blocks, batch, reverse):
    rows = S5_TIME_BLOCK * batch
    nb = u.shape[0] // rows
    if reverse:
        blk = lambda i: (nb - 1 - i, 0)
    else:
        blk = lambda i: ((i + n_first_blocks) % nb, 0)
    accumulate = y_in is not None
    const3 = lambda i: (0, 0, 0)
    in_specs = [pl.BlockSpec((rows, D_MODEL), blk)]
    args = [u]
    if accumulate:
        in_specs.append(pl.BlockSpec((rows, D_MODEL), blk))
        args.append(y_in)
    in_specs += [pl.BlockSpec(w1.shape, const3), pl.BlockSpec(w3.shape, const3), pl.BlockSpec(a.shape, const3)]
    args += [w1, w3, a]
    return pl.pallas_call(
        functools.partial(_s5_scan_kernel, tt=S5_TIME_BLOCK, batch=batch, reverse=reverse, accumulate=accumulate),
        grid=(nb,),
        in_specs=in_specs,
        out_specs=pl.BlockSpec((rows, D_MODEL), blk),
        out_shape=jax.ShapeDtypeStruct((u.shape[0], D_MODEL), F32),
        scratch_shapes=[pltpu.VMEM((S5_SLABS, rows, 2 * S5_SLAB_STATE), F32),
                        pltpu.VMEM((S5_SLABS, batch, 2 * S5_SLAB_STATE), F32)],
        input_output_aliases={1: 0} if accumulate else {},
        compiler_params=_params("arbitrary"),
        name="s5_scan_bwd" if reverse else "s5_scan_fwd",
    )(*args)


def _s5_out_kernel(x_ref, y_ref, g_ref, mod_ref, d_ref, w_ref, o_ref):
    x = x_ref[...]
    h = _norm_mod(x, g_ref[...], mod_ref[0, 0:1, :], mod_ref[0, 1:2, :])
    z = jax.nn.gelu(y_ref[...] + d_ref[...] * h, approximate=True)
    gg = jnp.dot(z.astype(BF16), w_ref[...], preferred_element_type=F32)
    o = gg[:, :D_MODEL] * jax.nn.sigmoid(gg[:, D_MODEL:])
    o_ref[...] = x + mod_ref[0, 2:3, :] * o


def _s5_out(x, y_tm, g, mod, d, w_glu, bsz, n, token_offset):
    tm = _pick(n, 512)
    nblk = n // tm
    assert token_offset % tm == 0
    off = token_offset // tm
    return pl.pallas_call(
        _s5_out_kernel,
        grid=(bsz, nblk),
        in_specs=[pl.BlockSpec((tm, D_MODEL), lambda b, i: (b * nblk + i, 0)),
                  pl.BlockSpec((tm, D_MODEL), lambda b, i: (off + i, b)),
                  pl.BlockSpec((1, D_MODEL), lambda b, i: (0, 0)),
                  _batch_mod_spec(mod),
                  pl.BlockSpec((1, D_MODEL), lambda b, i: (0, 0)),
                  pl.BlockSpec((D_MODEL, 2 * D_MODEL), lambda b, i: (0, 0))],
        out_specs=pl.BlockSpec((tm, D_MODEL), lambda b, i: (b * nblk + i, 0)),
        out_shape=jax.ShapeDtypeStruct((bsz * n, D_MODEL), F32),
        compiler_params=_params("parallel", "parallel"),
        name="s5_glu_out",
    )(x, y_tm, g, mod, d, w_glu)


def _s5_operators(lam_re, lam_im, log_dt, b_re, b_im, c_re, c_im):
    lam_re = lam_re.astype(F32)
    lam_im = lam_im.astype(F32)
    dt = jnp.exp(log_dt.astype(F32))[..., None]
    mag = jnp.exp(lam_re * dt)
    a_re = mag * jnp.cos(lam_im * dt)
    a_im = mag * jnp.sin(lam_im * dt)
    n_re = a_re - 1.0
    n_im = a_im
    den = lam_re * lam_re + lam_im * lam_im
    k_re = (n_re * lam_re + n_im * lam_im) / den
    k_im = (n_im * lam_re - n_re * lam_im) / den
    b_re = b_re.astype(F32)
    b_im = b_im.astype(F32)
    bb_re = k_re[..., None] * b_re - k_im[..., None] * b_im
    bb_im = k_re[..., None] * b_im + k_im[..., None] * b_re
    sg, p, h = S5_SLAB_GROUPS, S5_STATE, S5_GROUP
    eye = jnp.eye(sg, dtype=F32)
    bb = jnp.stack([bb_re, bb_im], axis=1).reshape(2, 2, S5_SLABS, sg, p, h)
    bb = bb.transpose(0, 2, 3, 5, 1, 4)
    w1 = bb[:, :, :, :, :, None, :] * eye[None, None, :, None, None, :, None]
    w1 = w1.reshape(2, S5_SLABS, S5_SLAB_CH, 2 * S5_SLAB_STATE)
    cc = jnp.stack([c_re.astype(F32), -c_im.astype(F32)], axis=1).reshape(2, 2, S5_SLABS, sg, h, p)
    cc = cc.transpose(0, 2, 1, 5, 3, 4)
    w3 = cc[:, :, :, None, :, :, :] * eye[None, None, None, :, None, :, None]
    w3 = w3.reshape(2, S5_SLABS, 2 * S5_SLAB_STATE, S5_SLAB_CH)
    a = jnp.stack([a_re, a_im], axis=1).reshape(2, 2, S5_SLABS, S5_SLAB_STATE).transpose(0, 2, 1, 3)
    return w1.astype(BF16), w3.astype(BF16), a


def _s5_layer(x_l, x_c, g, mod_l, mod_c, ops, d, w_glu, bsz, n_lat, n_ctx, ctx_out):
    w1, w3, a = ops
    u_l = _norm_mod_time_major(x_l, g, mod_l, bsz, n_lat)
    u_c = _norm_mod_time_major(x_c, g, mod_c, bsz, n_ctx)
    n_tok = n_lat + n_ctx
    u = jnp.concatenate([u_l, u_c], axis=0).reshape(n_tok * bsz, D_MODEL)
    n_lat_blocks = n_lat // S5_TIME_BLOCK
    y = _s5_scan(u, None, w1[0], w3[0], a[0], n_lat_blocks, bsz, False)
    y = _s5_scan(u, y, w1[1], w3[1], a[1], n_lat_blocks, bsz, True)
    y = y.reshape(n_tok, bsz * D_MODEL)
    x_l = _s5_out(x_l, y, g, mod_l, d, w_glu, bsz, n_lat, 0)
    if ctx_out:
        x_c = _s5_out(x_c, y, g, mod_c, d, w_glu, bsz, n_ctx, n_lat)
    return x_l, x_c


def _lane_order():
    which, half, axis, f = np.meshgrid(np.arange(2), np.arange(2), np.arange(2), np.arange(16), indexing="ij")
    head_dim = (axis * 32 + which * 16 + f).reshape(-1)
    return (half.reshape(-1) * DIFF_HEAD_DIM + head_dim), head_dim


def _first_half_mask(shape):
    lane = lax.broadcasted_iota(jnp.int32, shape, len(shape) - 1)
    return (lane % 64) < 32


def _qkv_kernel(x_ref, g_ref, mod_ref, w_ref, gs_ref, ones_ref, cos_ref, sin_ref, o_ref, h_scr, *, rope):
    j = pl.program_id(1)

    @pl.when(j == 0)
    def _():
        h = _norm_mod(x_ref[...], g_ref[...], mod_ref[0, 0:1, :], mod_ref[0, 1:2, :])
        h_scr[...] = h.astype(BF16)

    y = jnp.dot(h_scr[...], w_ref[...], preferred_element_type=F32)

    @pl.when(j == 2)
    def _():
        o_ref[0] = y.astype(BF16)

    @pl.when(j < 2)
    def _():
        gs = gs_ref[0]
        ones = ones_ref[...]
        for hh in range(DIFF_HEADS):
            t = y[:, hh * LANES:(hh + 1) * LANES]
            t2 = t * t
            hi = t2.astype(BF16)
            lo = (t2 - hi.astype(F32)).astype(BF16)
            ss = (jnp.dot(hi, ones, preferred_element_type=F32)
                  + jnp.dot(lo, ones, preferred_element_type=F32))
            t = t * lax.rsqrt(ss * (1.0 / DIFF_HEAD_DIM) + NORM_EPS) * gs
            if rope:
                t = t * cos_ref[...] + pltpu.roll(t, LANES // 2, 1) * sin_ref[...]
            o_ref[0, :, hh * LANES:(hh + 1) * LANES] = t.astype(BF16)


def _qkv(x, g, mod, w_qkv, gs, ones, cos_t, sin_t, rows_per_batch, rope):
    m = x.shape[0]
    tm = _pick(rows_per_batch, 512)
    nblk = rows_per_batch // tm
    return pl.pallas_call(
        functools.partial(_qkv_kernel, rope=rope),
        grid=(m // tm, 3),
        in_specs=[pl.BlockSpec((tm, D_MODEL), lambda i, j: (i, 0)),
                  pl.BlockSpec((1, D_MODEL), lambda i, j: (0, 0)),
                  _mod_spec(mod, rows_per_batch, tm),
                  pl.BlockSpec((D_MODEL, D_MODEL), lambda i, j: (0, j)),
                  pl.BlockSpec((1, 1, LANES), lambda i, j: (jnp.minimum(j, 1), 0, 0)),
                  pl.BlockSpec((LANES, LANES), lambda i, j: (0, 0)),
                  pl.BlockSpec((tm, LANES), lambda i, j: (i % nblk, 0)),
                  pl.BlockSpec((tm, LANES), lambda i, j: (i % nblk, 0))],
        out_specs=pl.BlockSpec((1, tm, D_MODEL), lambda i, j: (j, i, 0)),
        out_shape=jax.ShapeDtypeStruct((3, m, D_MODEL), BF16),
        scratch_shapes=[pltpu.VMEM((tm, D_MODEL), BF16)],
        compiler_params=_params("parallel", "arbitrary"),
        name="diff_qkv",
    )(x, g, mod, w_qkv, gs, ones, cos_t, sin_t)


def _flash_kernel(stab_ref, q_ref, k_ref, v_ref, lam_ref, sub_ref, o_ref, vaug_scr, *, tk, lam_init, online):
    tq = q_ref.shape[1]
    nk = k_ref.shape[1]
    n_chunks = nk // tk

    @pl.when(pl.program_id(2) == 0)
    def _():
        vaug_scr[:, :DIFF_V_DIM] = v_ref[0]
        vaug_scr[:, DIFF_V_DIM:] = jnp.ones((nk, DIFF_V_DIM), BF16)

    q = q_ref[0]
    first = _first_half_mask(q.shape)
    zero = jnp.zeros_like(q)
    qq = jnp.concatenate([jnp.where(first, q, zero), jnp.where(first, zero, q)], axis=0)

    def scores(j):
        off = j * tk if isinstance(j, int) else pl.multiple_of(j * tk, tk)
        kk = k_ref[0, pl.ds(off, tk), :]
        return lax.dot_general(qq, kk, (((1,), (1,)), ((), ())), preferred_element_type=F32)

    if online:
        def update(s, j, state):
            m, acc = state
            off = j * tk if isinstance(j, int) else pl.multiple_of(j * tk, tk)
            m_new = jnp.maximum(m, jnp.max(s, axis=-1, keepdims=True))
            alpha = jnp.exp2(m - m_new)
            p = jnp.exp2(s - m_new).astype(BF16)
            acc = alpha * acc + jnp.dot(p, vaug_scr[pl.ds(off, tk), :], preferred_element_type=F32)
            return m_new, acc
        state = (jnp.full((2 * tq, 1), -jnp.inf, F32), jnp.zeros((2 * tq, 2 * DIFF_V_DIM), F32))
    else:
        stab = stab_ref[...]

        def update(s, j, acc):
            off = j * tk if isinstance(j, int) else pl.multiple_of(j * tk, tk)
            p = jnp.exp2(s - stab).astype(BF16)
            return acc + jnp.dot(p, vaug_scr[pl.ds(off, tk), :], preferred_element_type=F32)
        state = jnp.zeros((2 * tq, 2 * DIFF_V_DIM), F32)

    if online:
        def pair(jj, carry):
            s_a, st = carry
            j0 = 2 * jj
            s_b = scores(j0 + 1)
            st = update(s_a, j0, st)
            s_a = scores(j0 + 2)
            st = update(s_b, j0 + 1, st)
            return s_a, st

        n_pairs = (n_chunks - 1) // 2
        s_a, state = lax.fori_loop(0, n_pairs, pair, (scores(0), state))
        done = 2 * n_pairs
        if n_chunks - done == 2:
            s_b = scores(done + 1)
            state = update(s_a, done, state)
            state = update(s_b, done + 1, state)
        else:
            state = update(s_a, done, state)
        acc = state[1]
    else:
        s_cur = scores(0)
        for j in range(n_chunks):
            s_next = scores(j + 1) if j + 1 < n_chunks else None
            state = update(s_cur, j, state)
            s_cur = s_next
        acc = state

    lp = lam_ref[...]
    lam = (jnp.exp(jnp.sum(lp[0:1] * lp[1:2], axis=-1, keepdims=True))
           - jnp.exp(jnp.sum(lp[2:3] * lp[3:4], axis=-1, keepdims=True)) + lam_init)
    o = (acc[:tq, :DIFF_V_DIM] / acc[:tq, DIFF_V_DIM:]
         - lam * (acc[tq:, :DIFF_V_DIM] / acc[tq:, DIFF_V_DIM:]))
    ms = jnp.mean(o * o, axis=-1, keepdims=True)
    o = o * lax.rsqrt(ms + NORM_EPS) * sub_ref[...] * (1.0 - lam_init)
    o_ref[0] = o.astype(o_ref.dtype)


def _flash(stab, q, k, v, lam_params, subln, lam_init, online):
    bsz, nq, _ = q.shape
    nk = k.shape[1]
    tq = _pick(nq, 512)
    tk = _pick(nk, 256)
    return pl.pallas_call(
        functools.partial(_flash_kernel, tk=tk, lam_init=lam_init, online=online),
        grid=(bsz, DIFF_HEADS, nq // tq),
        in_specs=[pl.BlockSpec((1, 1), lambda b, h, i: (0, 0)),
                  pl.BlockSpec((1, tq, LANES), lambda b, h, i: (b, i, h)),
                  pl.BlockSpec((1, nk, LANES), lambda b, h, i: (b, 0, h)),
                  pl.BlockSpec((1, nk, LANES), lambda b, h, i: (b, 0, h)),
                  pl.BlockSpec((4, DIFF_HEAD_DIM), lambda b, h, i: (0, 0)),
                  pl.BlockSpec((1, DIFF_V_DIM), lambda b, h, i: (0, 0))],
        out_specs=pl.BlockSpec((1, tq, LANES), lambda b, h, i: (b, i, h)),
        out_shape=jax.ShapeDtypeStruct((bsz, nq, D_MODEL), BF16),
        scratch_shapes=[pltpu.VMEM((nk, 2 * DIFF_V_DIM), BF16)],
        compiler_params=_params("parallel", "parallel", "arbitrary"),
        name="diff_flash_attention",
    )(stab, q, k, v, lam_params, subln)


_MAX_FIXED_STABILISER = 40.0


def _attend(stab, q, k, v, lam_params, subln, lam_init):
    return lax.cond(stab[0, 0] <= _MAX_FIXED_STABILISER,
                    lambda: _flash(stab, q, k, v, lam_params, subln, lam_init, False),
                    lambda: _flash(stab, q, k, v, lam_params, subln, lam_init, True))


def _rope_tables(n_tokens):
    rows = n_tokens // GRID_W
    row = jnp.repeat(jnp.arange(rows, dtype=F32), GRID_W)
    col = jnp.tile(jnp.arange(GRID_W, dtype=F32), rows)
    half = DIFF_HEAD_DIM // 2
    inv = jnp.power(ROPE_BASE, -jnp.arange(0, half, 2, dtype=F32) / half)
    ang = jnp.concatenate([row[:, None] * inv, col[:, None] * inv], axis=-1)
    ang = jnp.tile(ang, (1, 4))
    sign = jnp.asarray(np.repeat([-1.0, 1.0], LANES // 2), F32)
    return jnp.cos(ang), jnp.sin(ang) * sign


def _diff_layer(x_l, x_c, g, mod_l, mod_c, w_qkv, q_norm, k_norm, lam_params, subln, w_o,
                lam_init, bsz, n_lat, n_ctx, ctx_out):
    cos_t, sin_t = _rope_tables(n_lat)
    tile_src, head_dim = _lane_order()
    q_scale = DIFF_HEAD_DIM ** -0.5 * math.log2(math.e)
    qn = q_norm.astype(F32)
    kn = k_norm.astype(F32)
    gs = jnp.stack([qn[head_dim] * q_scale, kn[head_dim]])[:, None, :]
    half_id = (np.arange(LANES) % 64) // 32
    ones = jnp.asarray(half_id[:, None] == half_id[None, :], F32).astype(BF16)
    stab = (1.02 * DIFF_HEAD_DIM * q_scale * jnp.max(jnp.abs(qn)) * jnp.max(jnp.abs(kn))).reshape(1, 1)

    def permuted(w):
        return w.reshape(D_MODEL, DIFF_HEADS, LANES)[:, :, tile_src].reshape(D_MODEL, D_MODEL)
    w = jnp.concatenate([permuted(w_qkv[:, :D_MODEL]), permuted(w_qkv[:, D_MODEL:2 * D_MODEL]),
                         w_qkv[:, 2 * D_MODEL:]], axis=1).astype(BF16)
    qkv_l = _qkv(x_l, g, mod_l, w, gs, ones, cos_t, sin_t, n_lat, True).reshape(3, bsz, n_lat, D_MODEL)
    qkv_c = _qkv(x_c, g, mod_c, w, gs, ones, cos_t, sin_t, n_ctx, False).reshape(3, bsz, n_ctx, D_MODEL)
    k_all = jnp.concatenate([qkv_l[1], qkv_c[1]], axis=1)
    v_all = jnp.concatenate([qkv_l[2], qkv_c[2]], axis=1)
    lam_p = lam_params.astype(F32)
    sub = subln.astype(F32)[None]
    wo = w_o.astype(BF16)
    zero_b = jnp.zeros((1, D_MODEL), F32)
    o_l = _attend(stab, qkv_l[0], k_all, v_all, lam_p, sub, lam_init).reshape(bsz * n_lat, D_MODEL)
    x_l = _linear_residual(o_l, wo, zero_b, x_l, mod_l, n_lat, 2)
    if ctx_out:
        o_c = _attend(stab, qkv_c[0], qkv_c[1], qkv_c[2], lam_p, sub, lam_init).reshape(bsz * n_ctx, D_MODEL)
        x_c = _linear_residual(o_c, wo, zero_b, x_c, mod_c, n_ctx, 2)
    return x_l, x_c


def _fft1_kernel(x_ref, g_ref, mod_ref, t_ref, o_ref, *, n1, group):
    gamma = g_ref[...]
    shift = mod_ref[0, 0:1, :]
    scale = mod_ref[0, 1:2, :]
    for jj in range(group):
        sl = slice(jj * D_MODEL, (jj + 1) * D_MODEL)
        h = _norm_mod(x_ref[0, :, sl], gamma, shift, scale).astype(BF16)
        a = jnp.dot(t_ref[jj], h, preferred_element_type=F32)
        o_ref[0, 0, :, sl] = a[:n1].astype(BF16)
        o_ref[0, 1, :, sl] = a[n1:].astype(BF16)


def _fft2_kernel(a_ref, x_ref, mod_ref, f2_ref, cd_ref, w_ref, b_ref, o_ref, zr_scr, zi_scr, *, n2, group):
    for kk in range(group):
        a = jnp.concatenate([a_ref[0, 0, kk], a_ref[0, 1, kk]], axis=0)
        z = jnp.dot(f2_ref[...], a, preferred_element_type=F32)
        zr_scr[kk * n2:(kk + 1) * n2, :] = z[:n2].astype(BF16)
        zi_scr[kk * n2:(kk + 1) * n2, :] = z[n2:].astype(BF16)
    zr = zr_scr[...]
    zi = zi_scr[...]
    fs = []
    for cg in range(FOURIER_GROUPS):
        sl = slice(cg * FOURIER_CH, (cg + 1) * FOURIER_CH)
        fs.append(jnp.dot(zr[:, sl], cd_ref[0], preferred_element_type=F32)
                  + jnp.dot(zi[:, sl], cd_ref[1], preferred_element_type=F32))
    f = jnp.concatenate(fs, axis=-1).astype(BF16)
    o = jnp.dot(f, w_ref[...], preferred_element_type=F32) + b_ref[...]
    gate = mod_ref[0, 2:3, :]
    for kk in range(group):
        sl = slice(kk * D_MODEL, (kk + 1) * D_MODEL)
        o_ref[0, :, sl] = x_ref[0, :, sl] + gate * o[kk * n2:(kk + 1) * n2]


def _fourier_tables(n):
    n1 = int(round(math.sqrt(n)))
    n2 = n // n1
    assert n1 * n2 == n and n1 % 8 == 0 and n2 % 8 == 0
    k1 = np.arange(n1)[None, :, None]
    t1 = np.arange(n1)[None, None, :]
    t2 = np.arange(n2)[:, None, None]
    ang = 2.0 * np.pi * ((k1 * (n2 * t1 + t2)) % n) / n
    t_tab = np.concatenate([np.cos(ang), -np.sin(ang)], axis=1)
    k2 = np.arange(n2)[:, None]
    tt = np.arange(n2)[None, :]
    ang2 = 2.0 * np.pi * ((k2 * tt) % n2) / n2
    c2, s2 = np.cos(ang2), np.sin(ang2)
    f2 = np.block([[c2, s2], [-s2, c2]])
    cc = np.arange(FOURIER_CH)
    ang3 = 2.0 * np.pi * ((cc[:, None] * cc[None, :]) % FOURIER_CH) / FOURIER_CH
    norm = 1.0 / math.sqrt(n * FOURIER_CH)
    cd = np.stack([np.cos(ang3), np.sin(ang3)]) * norm
    as_bf16 = lambda z: jnp.asarray(z, dtype=F32).astype(BF16)
    return n1, n2, as_bf16(t_tab), as_bf16(f2), as_bf16(cd)


def _fourier_layer_one(x, g, mod, w_f, b_f, bsz, n):
    n1, n2, t_tab, f2, cd = _fourier_tables(n)
    g1 = _pick(n2, 8)
    g2 = _pick(n1, 8)
    mod_spec = _batch_mod_spec(mod)
    a = pl.pallas_call(
        functools.partial(_fft1_kernel, n1=n1, group=g1),
        grid=(bsz, n2 // g1),
        in_specs=[pl.BlockSpec((1, n1, g1 * D_MODEL), lambda b, j: (b, 0, j)),
                  pl.BlockSpec((1, D_MODEL), lambda b, j: (0, 0)),
                  mod_spec,
                  pl.BlockSpec((g1, 2 * n1, n1), lambda b, j: (j, 0, 0))],
        out_specs=pl.BlockSpec((1, 2, n1, g1 * D_MODEL), lambda b, j: (b, 0, 0, j)),
        out_shape=jax.ShapeDtypeStruct((bsz, 2, n1, n2 * D_MODEL), BF16),
        compiler_params=_params("parallel", "parallel"),
        name="fourier_token_dft_stage1",
    )(x.reshape(bsz, n1, n2 * D_MODEL), g, mod, t_tab)
    a = a.reshape(bsz, 2, n1, n2, D_MODEL)
    out = pl.pallas_call(
        functools.partial(_fft2_kernel, n2=n2, group=g2),
        grid=(bsz, n1 // g2),
        in_specs=[pl.BlockSpec((1, 2, g2, n2, D_MODEL), lambda b, j: (b, 0, j, 0, 0)),
                  pl.BlockSpec((1, n2, g2 * D_MODEL), lambda b, j: (b, 0, j)),
                  mod_spec,
                  pl.BlockSpec((2 * n2, 2 * n2), lambda b, j: (0, 0)),
                  pl.BlockSpec((2, FOURIER_CH, FOURIER_CH), lambda b, j: (0, 0, 0)),
                  pl.BlockSpec((D_MODEL, D_MODEL), lambda b, j: (0, 0)),
                  pl.BlockSpec((1, D_MODEL), lambda b, j: (0, 0))],
        out_specs=pl.BlockSpec((1, n2, g2 * D_MODEL), lambda b, j: (b, 0, j)),
        out_shape=jax.ShapeDtypeStruct((bsz, n2, n1 * D_MODEL), F32),
        scratch_shapes=[pltpu.VMEM((g2 * n2, D_MODEL), BF16), pltpu.VMEM((g2 * n2, D_MODEL), BF16)],
        compiler_params=_params("parallel", "parallel"),
        name="fourier_token_dft_stage2",
    )(a, x.reshape(bsz, n2, n1 * D_MODEL), mod, f2, cd, w_f, b_f)
    return out.reshape(bsz * n, D_MODEL)


def kernel(x, c, ctx, c_ctx, w_mod, b_mod, norm_g, mlp_w1, mlp_w2, s5_lambda_re, s5_lambda_im, s5_log_dt, s5_b_re, s5_b_im, s5_c_re, s5_c_im, s5_d, s5_w_glu, diff_w_qkv, diff_q_norm, diff_k_norm, diff_lambda, diff_subln, diff_w_o, fourier_w, fourier_b):
    bsz, n_lat, _ = x.shape
    n_ctx = ctx.shape[1]
    depth = w_mod.shape[0]
    assert bsz == 8 and n_lat % GRID_W == 0 and n_ctx % S5_TIME_BLOCK == 0 and n_lat % S5_TIME_BLOCK == 0

    cond_rows = 16
    cond = jnp.concatenate([c.astype(F32), c_ctx.astype(F32)[None],
                            jnp.zeros((cond_rows - bsz - 1, D_MODEL), F32)], axis=0)
    mod_all = _modulation(cond, w_mod, b_mod).reshape(depth, cond_rows, N_MOD, D_MODEL)

    x_l = x.reshape(bsz * n_lat, D_MODEL)
    x_c = ctx.reshape(bsz * n_ctx, D_MODEL)
    for i in range(depth):
        last = i == depth - 1
        mod_l = mod_all[i, :bsz]
        mod_c = mod_all[i, bsz:bsz + 1]
        g_mix = norm_g[i, 0][None]
        g_mlp = norm_g[i, 1][None]
        kind = i % 3
        j = i // 3
        if kind == 0:
            ops = _s5_operators(s5_lambda_re[j], s5_lambda_im[j], s5_log_dt[j], s5_b_re[j], s5_b_im[j],
                                s5_c_re[j], s5_c_im[j])
            x_l, x_c = _s5_layer(x_l, x_c, g_mix, mod_l, mod_c, ops, s5_d[j][None],
                                 s5_w_glu[j].astype(BF16), bsz, n_lat, n_ctx, not last)
        elif kind == 1:
            lam_init = 0.8 - 0.6 * math.exp(-0.3 * i)
            x_l, x_c = _diff_layer(x_l, x_c, g_mix, mod_l, mod_c, diff_w_qkv[j], diff_q_norm[j],
                                   diff_k_norm[j], diff_lambda[j], diff_subln[j], diff_w_o[j],
                                   lam_init, bsz, n_lat, n_ctx, not last)
        else:
            w_f = fourier_w[j].astype(BF16)
            b_f = fourier_b[j][None]
            x_l = _fourier_layer_one(x_l, g_mix, mod_l, w_f, b_f, bsz, n_lat)
            if not last:
                x_c = _fourier_layer_one(x_c, g_mix, mod_c, w_f, b_f, bsz, n_ctx)
        w1 = mlp_w1[i].astype(BF16)
        w2 = mlp_w2[i].astype(BF16)
        x_l = _mlp(x_l, g_mlp, mod_l, w1, w2, n_lat)
        if not last:
            x_c = _mlp(x_c, g_mlp, mod_c, w1, w2, n_ctx)
    return x_l.reshape(bsz, n_lat, D_MODEL)
```
